```python
import math
import jax, jax.numpy as jnp
from jax import lax
import numpy as np

D_MODEL = 1024
BATCH = 16
SEQ = 2048
DEPTH = 1

RET_HEADS = 4
RET_DK = 128
RET_DV = 128
RET_WIDTH = RET_HEADS * RET_DV
RET_CHUNK = 128
ROPE_BASE = 10000.0
DIFF_HEADS = 4
DIFF_DK = 64
DIFF_DV = 128
DIFF_WIDTH = DIFF_HEADS * DIFF_DV
Q_BLOCK = 128
MIX_WIDTH = RET_WIDTH + DIFF_WIDTH
COL_SIZES = (RET_HEADS * RET_DK, RET_HEADS * RET_DK, RET_WIDTH, RET_WIDTH,
             DIFF_HEADS * 2 * DIFF_DK, DIFF_HEADS * 2 * DIFF_DK, DIFF_WIDTH)
IN_COLS = sum(COL_SIZES)
COL_SPLITS = tuple(int(s) for s in np.cumsum(COL_SIZES)[:-1])
PEER_HEADS = 8
PEER_N_KEYS = 128
PEER_N_EXPERTS = PEER_N_KEYS * PEER_N_KEYS
PEER_D_QUERY = 256
PEER_D_HALF = PEER_D_QUERY // 2
PEER_TOPK = 16
PEER_TOKEN_CHUNK = 128
DEEPNORM_ALPHA = (2.0 * DEPTH) ** 0.25
DEEPNORM_BETA = (8.0 * DEPTH) ** -0.25
LN_EPS = 1e-5

kernel_name = "hymba_retnet_diffattn_peer_deepnorm"


def lambda_init(layer_idx):
    return 0.8 - 0.6 * math.exp(-0.3 * (layer_idx - 1))


def layer_norm(x, g, b):
    xf = x.astype(jnp.float32)
    mu = jnp.mean(xf, -1, keepdims=True)
    var = jnp.mean(jnp.square(xf - mu), -1, keepdims=True)
    return ((xf - mu) * lax.rsqrt(var + LN_EPS) * g + b).astype(x.dtype)


def rotary(x, pos):
    half = x.shape[-1] // 2
    inv = ROPE_BASE ** (-jnp.arange(half, dtype=jnp.float32) / half)
    ang = pos.astype(jnp.float32)[:, None] * inv[None, :]
    cos = jnp.cos(ang).astype(x.dtype)
    sin = jnp.sin(ang).astype(x.dtype)
    x1, x2 = x[..., :half], x[..., half:]
    return jnp.concatenate([x1 * cos - x2 * sin, x1 * sin + x2 * cos], axis=-1)


def retention_chunkwise(q, k, v):
    B, H, S, dk = q.shape
    dv = v.shape[-1]
    C = RET_CHUNK
    n = S // C
    qf, kf, vf = q.astype(jnp.float32), k.astype(jnp.float32) * dk ** -0.5, v.astype(jnp.float32)
    log_g = jnp.log1p(-(2.0 ** (-5.0 - jnp.arange(H, dtype=jnp.float32))))
    idx = jnp.arange(C, dtype=jnp.float32)
    rel = idx[:, None] - idx[None, :]
    decay_in = jnp.where(rel[None] >= 0, jnp.exp(jnp.maximum(rel, 0.0)[None] * log_g[:, None, None]), 0.0)
    decay_q = jnp.exp((idx + 1.0)[None, :] * log_g[:, None])
    decay_k = jnp.exp((C - 1.0 - idx)[None, :] * log_g[:, None])
    decay_c = jnp.exp(C * log_g)

    def to_chunks(t):
        return t.reshape(B, H, n, C, t.shape[-1]).transpose(2, 0, 1, 3, 4)

    def step(state, inp):
        qi, ki, vi = inp
        s = jnp.einsum('bhid,bhjd->bhij', qi, ki) * decay_in[None]
        inner = jnp.einsum('bhij,bhjv->bhiv', s, vi)
        cross = jnp.einsum('bhid,bhdv->bhiv', qi, state) * decay_q[None, :, :, None]
        state = state * decay_c[None, :, None, None] + jnp.einsum(
            'bhjd,bhjv->bhdv', ki * decay_k[None, :, :, None], vi)
        return state, inner + cross

    state0 = jnp.zeros((B, H, dk, dv), jnp.float32)
    _, out = lax.scan(step, state0, (to_chunks(qf), to_chunks(kf), to_chunks(vf)))
    return out.transpose(1, 2, 0, 3, 4).reshape(B, H, S, dv).astype(q.dtype)


def diff_attention(q, k, v, lam):
    B, H, _, S, dk = q.shape
    dv = v.shape[-1]
    nb = S // Q_BLOCK
    scale = dk ** -0.5
    kpos = jnp.arange(S)
    qb = q.reshape(B, H, 2, nb, Q_BLOCK, dk).transpose(3, 0, 1, 2, 4, 5)

    def block(args):
        qi, i = args
        s = jnp.einsum('bhcqd,bhckd->bhcqk', qi, k).astype(jnp.float32) * scale
        qpos = i * Q_BLOCK + jnp.arange(Q_BLOCK)
        mask = kpos[None, :] <= qpos[:, None]
        p = jax.nn.softmax(jnp.where(mask, s, -jnp.inf), axis=-1)
        a = p[:, :, 0] - lam * p[:, :, 1]
        return jnp.einsum('bhqk,bhkv->bhqv', a.astype(v.dtype), v)

    out = lax.map(block, (qb, jnp.arange(nb)))
    return out.transpose(1, 2, 0, 3, 4).reshape(B, H, S, dv)


def head_norm(o, g, b, rms_only):
    B, H, S, d = o.shape
    of = o.astype(jnp.float32)
    if rms_only:
        on = of * lax.rsqrt(jnp.mean(jnp.square(of), -1, keepdims=True) + LN_EPS)
    else:
        mu = jnp.mean(of, -1, keepdims=True)
        var = jnp.mean(jnp.square(of - mu), -1, keepdims=True)
        on = (of - mu) * lax.rsqrt(var + LN_EPS)
    on = on.transpose(0, 2, 1, 3).reshape(B, S, H * d) * g
    if b is not None:
        on = on + b
    return on.astype(o.dtype)


def hybrid_mixer(x, w_in, ret_gn_g, ret_gn_b, lq1, lk1, lq2, lk2, subln_g, w_out, lam_init):
    B, S, _ = x.shape
    pos = jnp.arange(S)
    proj = x @ w_in
    rq, rk, rv, rg, dq, dk, dv = jnp.split(proj, COL_SPLITS, axis=-1)
    heads = lambda t, d: t.reshape(B, S, -1, d).transpose(0, 2, 1, 3)
    r_out = retention_chunkwise(rotary(heads(rq, RET_DK), pos), rotary(heads(rk, RET_DK), pos),
                                heads(rv, RET_DV))
    r = jax.nn.silu(rg) * head_norm(r_out, ret_gn_g, ret_gn_b, rms_only=False)
    dq = dq.reshape(B, S, DIFF_HEADS, 2, DIFF_DK).transpose(0, 2, 3, 1, 4)
    dk = dk.reshape(B, S, DIFF_HEADS, 2, DIFF_DK).transpose(0, 2, 3, 1, 4)
    f32 = jnp.float32
    lam = (jnp.exp(jnp.sum(lq1.astype(f32) * lk1.astype(f32)))
           - jnp.exp(jnp.sum(lq2.astype(f32) * lk2.astype(f32))) + lam_init)
    d_out = diff_attention(dq, dk, heads(dv, DIFF_DV), lam)
    d = head_norm(d_out, subln_g, None, rms_only=True) * (1.0 - lam_init)
    return jnp.concatenate([r, d], axis=-1) @ w_out


def peer(x, w_query, sub_keys, expert_u, expert_v):
    B, S, D = x.shape
    T = B * S
    xt = x.reshape(T, D)
    q = (xt @ w_query).reshape(T, PEER_HEADS, 2, PEER_D_HALF)
    s = jnp.einsum('thcd,hckd->thck', q, sub_keys).astype(jnp.float32)
    top_s, top_i = lax.top_k(s, PEER_TOPK)
    cand_s = (top_s[:, :, 0, :, None] + top_s[:, :, 1, None, :]).reshape(T, PEER_HEADS, PEER_TOPK ** 2)
    cand_i = (top_i[:, :, 0, :, None] * PEER_N_KEYS + top_i[:, :, 1, None, :]).reshape(T, PEER_HEADS, PEER_TOPK ** 2)
    best_s, pos = lax.top_k(cand_s, PEER_TOPK)
    expert_idx = jnp.take_along_axis(cand_i, pos, axis=-1)
    gate = jax.nn.softmax(best_s, axis=-1)
    C = PEER_TOKEN_CHUNK
    nc = T // C
    E = PEER_HEADS * PEER_TOPK

    def chunk(args):
        xc, ic, gc = args
        u = jnp.take(expert_u, ic, axis=0)
        h = jax.nn.gelu(jnp.einsum('cd,ced->ce', xc, u).astype(jnp.float32), approximate=False)
        vv = jnp.take(expert_v, ic, axis=0)
        return jnp.einsum('ce,ced->cd', (gc * h).astype(vv.dtype), vv)

    out = lax.map(chunk, (xt.reshape(nc, C, D), expert_idx.reshape(nc, C, E), gate.reshape(nc, C, E)))
    return out.reshape(B, S, D).astype(x.dtype)


def setup_inputs(seed: int = 0) -> dict:
    key = jax.random.key(seed)
    ks = jax.random.split(key, 24)
    nrm = lambda k, shape, s: jax.random.normal(k, shape, jnp.float32) * s
    L = DEPTH
    beta = DEEPNORM_BETA
    col_scale = jnp.concatenate([
        jnp.full((COL_SIZES[0] + COL_SIZES[1],), 1.0),
        jnp.full((COL_SIZES[2],), beta),
        jnp.full((COL_SIZES[3] + COL_SIZES[4] + COL_SIZES[5],), 1.0),
        jnp.full((COL_SIZES[6],), beta),
    ]).astype(jnp.float32)
    w_in = nrm(ks[1], (L, D_MODEL, IN_COLS), D_MODEL ** -0.5) * col_scale
    return {
        "x": nrm(ks[0], (BATCH, SEQ, D_MODEL), 1.0),
        "w_in": w_in,
        "ret_gn_g": 1.0 + nrm(ks[2], (L, RET_WIDTH), 0.02),
        "ret_gn_b": nrm(ks[3], (L, RET_WIDTH), 0.02),
        "diff_lambda_q1": nrm(ks[4], (L, DIFF_DK), 0.1),
        "diff_lambda_k1": nrm(ks[5], (L, DIFF_DK), 0.1),
        "diff_lambda_q2": nrm(ks[6], (L, DIFF_DK), 0.1),
        "diff_lambda_k2": nrm(ks[7], (L, DIFF_DK), 0.1),
        "diff_subln_g": 1.0 + nrm(ks[8], (L, DIFF_WIDTH), 0.02),
        "w_out": nrm(ks[9], (L, MIX_WIDTH, D_MODEL), MIX_WIDTH ** -0.5 * beta),
        "ln1_g": 1.0 + nrm(ks[10], (L, D_MODEL), 0.02),
        "ln1_b": nrm(ks[11], (L, D_MODEL), 0.02),
        "peer_w_query": nrm(ks[12], (L, D_MODEL, PEER_HEADS * PEER_D_QUERY), D_MODEL ** -0.5),
        "peer_sub_keys": nrm(ks[13], (L, PEER_HEADS, 2, PEER_N_KEYS, PEER_D_HALF), PEER_D_HALF ** -0.5),
        "peer_u": nrm(ks[14], (L, PEER_N_EXPERTS, D_MODEL), D_MODEL ** -0.5),
        "peer_v": nrm(ks[15], (L, PEER_N_EXPERTS, D_MODEL), beta),
        "ln2_g": 1.0 + nrm(ks[16], (L, D_MODEL), 0.02),
        "ln2_b": nrm(ks[17], (L, D_MODEL), 0.02),
    }


def reference(x, w_in, ret_gn_g, ret_gn_b, diff_lambda_q1, diff_lambda_k1, diff_lambda_q2,
              diff_lambda_k2, diff_subln_g, w_out, ln1_g, ln1_b, peer_w_query, peer_sub_keys,
              peer_u, peer_v, ln2_g, ln2_b):
    for l in range(DEPTH):
        mix = hybrid_mixer(x, w_in[l], ret_gn_g[l], ret_gn_b[l], diff_lambda_q1[l], diff_lambda_k1[l],
                           diff_lambda_q2[l], diff_lambda_k2[l], diff_subln_g[l], w_out[l],
                           lambda_init(l + 1))
        x = layer_norm(DEEPNORM_ALPHA * x + mix, ln1_g[l], ln1_b[l])
        ffn = peer(x, peer_w_query[l], peer_sub_keys[l], peer_u[l], peer_v[l])
        x = layer_norm(DEEPNORM_ALPHA * x + ffn, ln2_g[l], ln2_b[l])
    return x
```

```python
import functools
import math

import jax
import jax.numpy as jnp
import numpy as np
from jax import lax
from jax.experimental import pallas as pl
from jax.experimental.pallas import tpu as pltpu

F32 = jnp.float32
BF16 = jnp.bfloat16

V7X_LANES = 128
V7X_VMEM_BYTES = 64 * 1024 * 1024

RET_HEADS = 4
DIFF_HEADS = 4
HEAD_DIM = 128
RET_CHUNK = 128
ROPE_BASE = 10000.0
PEER_HEADS = 8
PEER_KEYS = 128
PEER_TOPK = 16
LN_EPS = 1e-5
GROUP = RET_HEADS * HEAD_DIM

INPROJ_ROWS = 512
RET_ROWS = 512
DIFF_QBLOCK = 256
OUTPROJ_ROWS = 512
ROUTE_TOKENS = 256
EXPERT_TOKENS = 512
EXPERT_BLOCK = 1024
GATE_ROWS = 32


def _vmem_limit(pipelined_bytes, resident_bytes):
    need = 2 * pipelined_bytes + resident_bytes
    return int(min(max(need, 16 * 1024 * 1024), V7X_VMEM_BYTES - 8 * 1024 * 1024))


def _nbytes(shape, dtype):
    return int(np.prod(shape)) * jnp.dtype(dtype).itemsize


def _rope_kernel(cos_ref, sin_ref):
    shape = cos_ref.shape
    half = HEAD_DIM // 2
    pos = lax.broadcasted_iota(jnp.int32, shape, 0).astype(F32)
    lane = lax.broadcasted_iota(jnp.int32, shape, 1)
    freq_idx = jnp.where(lane >= half, lane - half, lane).astype(F32)
    inv = jnp.exp(freq_idx * (-math.log(ROPE_BASE) / half))
    ang = pos * inv
    cos_ref[...] = jnp.cos(ang)
    sin = jnp.sin(ang)
    sin_ref[...] = jnp.where(lane >= half, sin, -sin)


def _rope_tables(seq):
    return pl.pallas_call(
        _rope_kernel,
        out_shape=(jax.ShapeDtypeStruct((seq, HEAD_DIM), F32),) * 2,
    )()


def _inproj_kernel(x_ref, w_ref, cos_ref, sin_ref, rq_ref, rk_ref, rv_ref, rg_ref, dq_ref, dk_ref, dv_ref):
    xb = x_ref[...].astype(BF16)
    cos = cos_ref[...]
    sin = sin_ref[...]

    def proj(group):
        return jnp.dot(xb, w_ref[:, group * GROUP:(group + 1) * GROUP], preferred_element_type=F32)

    def store_rotary(out_ref, p, scale):
        for h in range(RET_HEADS):
            ph = p[:, h * HEAD_DIM:(h + 1) * HEAD_DIM]
            rot = ph * cos + pltpu.roll(ph, HEAD_DIM // 2, axis=1) * sin
            if scale != 1.0:
                rot = rot * scale
            out_ref[:, h * HEAD_DIM:(h + 1) * HEAD_DIM] = rot.astype(out_ref.dtype)

    store_rotary(rq_ref, proj(0), 1.0)
    store_rotary(rk_ref, proj(1), HEAD_DIM ** -0.5)
    rv_ref[...] = proj(2).astype(rv_ref.dtype)
    rg_ref[...] = proj(3).astype(rg_ref.dtype)
    dq_ref[...] = (proj(4) * ((HEAD_DIM // 2) ** -0.5)).astype(dq_ref.dtype)
    dk_ref[...] = proj(5).astype(dk_ref.dtype)
    dv_ref[...] = proj(6).astype(dv_ref.dtype)


def _in_projection(x2d, w_in, cos_tab, sin_tab, seq):
    tokens, d_model = x2d.shape
    cols = w_in.shape[1]
    tm = INPROJ_ROWS
    pos_blocks = seq // tm
    out = jax.ShapeDtypeStruct((tokens, GROUP), BF16)
    group_spec = pl.BlockSpec((tm, GROUP), lambda i: (i, 0))
    tab_spec = pl.BlockSpec((tm, HEAD_DIM), lambda i: (i % pos_blocks, 0))
    return pl.pallas_call(
        _inproj_kernel,
        grid=(tokens // tm,),
        in_specs=[
            pl.BlockSpec((tm, d_model), lambda i: (i, 0)),
            pl.BlockSpec((d_model, cols), lambda i: (0, 0)),
            tab_spec,
            tab_spec,
        ],
        out_specs=[group_spec] * 7,
        out_shape=[out] * 7,
        compiler_params=pltpu.CompilerParams(
            dimension_semantics=("arbitrary",),
            vmem_limit_bytes=_vmem_limit(
                _nbytes((tm, d_model), F32) + _nbytes((d_model, cols), BF16) + 7 * _nbytes((tm, GROUP), BF16)
                + 2 * _nbytes((tm, HEAD_DIM), F32),
                _nbytes((tm, d_model), BF16) + 3 * _nbytes((tm, GROUP), F32)),
        ),
    )(x2d, w_in, cos_tab, sin_tab)


def _retention_kernel(q_ref, k_ref, v_ref, g_ref, gn_g_ref, gn_b_ref, o_ref, state_ref):
    @pl.when(pl.program_id(1) == 0)
    def _():
        state_ref[...] = jnp.zeros_like(state_ref)

    c = RET_CHUNK
    row = lax.broadcasted_iota(jnp.int32, (c, c), 0).astype(F32)
    col = lax.broadcasted_iota(jnp.int32, (c, c), 1).astype(F32)
    rel = row - col
    idx = lax.broadcasted_iota(jnp.int32, (c, 1), 0).astype(F32)
    for h in range(RET_HEADS):
        log_g = math.log1p(-(2.0 ** (-5.0 - h)))
        decay_in = jnp.where(rel >= 0, jnp.exp(jnp.maximum(rel, 0.0) * log_g), 0.0)
        decay_q = jnp.exp((idx + 1.0) * log_g)
        decay_k = jnp.exp((c - 1.0 - idx) * log_g)
        decay_c = math.exp(c * log_g)
        hs = slice(h * HEAD_DIM, (h + 1) * HEAD_DIM)
        gn_g = gn_g_ref[:, hs]
        gn_b = gn_b_ref[:, hs]
        state = state_ref[h]
        for ci in range(q_ref.shape[0] // c):
            rs = slice(ci * c, (ci + 1) * c)
            q = q_ref[rs, hs]
            k = k_ref[rs, hs]
            v = v_ref[rs, hs]
            s = lax.dot_general(q, k, (((1,), (1,)), ((), ())), preferred_element_type=F32) * decay_in
            inner = jnp.dot(s.astype(BF16), v, preferred_element_type=F32)
            cross = jnp.dot(q, state.astype(BF16), preferred_element_type=F32) * decay_q
            kd_t = (k.astype(F32) * decay_k).T.astype(BF16)
            state = state * decay_c + jnp.dot(kd_t, v, preferred_element_type=F32)
            o = inner + cross
            mu = jnp.mean(o, axis=-1, keepdims=True)
            cen = o - mu
            var = jnp.mean(cen * cen, axis=-1, keepdims=True)
            normed = cen * lax.rsqrt(var + LN_EPS) * gn_g + gn_b
            gate = g_ref[rs, hs].astype(F32)
            o_ref[rs, hs] = (gate * jax.nn.sigmoid(gate) * normed).astype(o_ref.dtype)
        state_ref[h] = state


def _retention(rq, rk, rv, rg, gn_g, gn_b, batch, seq):
    rows = RET_ROWS
    blocks = seq // rows
    spec = pl.BlockSpec((rows, GROUP), lambda b, c: (b * blocks + c, 0))
    par = pl.BlockSpec((1, GROUP), lambda b, c: (0, 0))
    return pl.pallas_call(
        _retention_kernel,
        grid=(batch, blocks),
        in_specs=[spec, spec, spec, spec, par, par],
        out_specs=spec,
        out_shape=jax.ShapeDtypeStruct(rq.shape, BF16),
        scratch_shapes=[pltpu.VMEM((RET_HEADS, HEAD_DIM, HEAD_DIM), F32)],
        compiler_params=pltpu.CompilerParams(
            dimension_semantics=("arbitrary", "arbitrary"),
            vmem_limit_bytes=_vmem_limit(5 * _nbytes((rows, GROUP), BF16), 4 * 1024 * 1024),
        ),
    )(rq, rk, rv, rg, gn_g, gn_b)


def _diff_kernel(q_ref, k_ref, v_ref, lq1_ref, lk1_ref, lq2_ref, lk2_ref, g_ref, o_ref, *, lam_init):
    f = lambda r: r[...].astype(F32)
    lam = (jnp.exp(jnp.sum(f(lq1_ref) * f(lk1_ref), axis=-1, keepdims=True))
           - jnp.exp(jnp.sum(f(lq2_ref) * f(lk2_ref), axis=-1, keepdims=True)) + lam_init)
    seq = q_ref.shape[0]
    bq = DIFF_QBLOCK
    half = HEAD_DIM // 2
    lane = lax.broadcasted_iota(jnp.int32, (bq, HEAD_DIM), 1)
    tri = (lax.broadcasted_iota(jnp.int32, (bq, bq), 1) <= lax.broadcasted_iota(jnp.int32, (bq, bq), 0))
    nt = (((1,), (1,)), ((), ()))
    zero = jnp.zeros((), BF16)
    for qi in range(seq // bq):
        past = qi * bq
        q = q_ref[past:past + bq, :]
        q_comp = (jnp.where(lane < half, q, zero), jnp.where(lane >= half, q, zero))
        k_diag = k_ref[past:past + bq, :]
        attn_diag = None
        attn_past = None
        for comp in range(2):
            s_diag = lax.dot_general(q_comp[comp], k_diag, nt, preferred_element_type=F32)
            s_diag = jnp.where(tri, s_diag, -jnp.inf)
            m = jnp.max(s_diag, axis=-1, keepdims=True)
            if past:
                s_past = lax.dot_general(q_comp[comp], k_ref[0:past, :], nt, preferred_element_type=F32)
                m = jnp.maximum(m, jnp.max(s_past, axis=-1, keepdims=True))
                e_past = jnp.exp(s_past - m)
            e_diag = jnp.exp(s_diag - m)
            denom = jnp.sum(e_diag, axis=-1, keepdims=True)
            if past:
                denom = denom + jnp.sum(e_past, axis=-1, keepdims=True)
            weight = (1.0 / denom) if comp == 0 else (-lam / denom)
            attn_diag = e_diag * weight if comp == 0 else attn_diag + e_diag * weight
            if past:
                attn_past = e_past * weight if comp == 0 else attn_past + e_past * weight
        o = jnp.dot(attn_diag.astype(BF16), v_ref[past:past + bq, :], preferred_element_type=F32)
        if past:
            o = o + jnp.dot(attn_past.astype(BF16), v_ref[0:past, :], preferred_element_type=F32)
        normed = o * lax.rsqrt(jnp.mean(o * o, axis=-1, keepdims=True) + LN_EPS)
        o_ref[past:past + bq, :] = (normed * g_ref[...] * (1.0 - lam_init)).astype(o_ref.dtype)


def _diff_attention(dq, dk, dv, lq1, lk1, lq2, lk2, subln_g, batch, seq, lam_init):
    spec = pl.BlockSpec((seq, HEAD_DIM), lambda b, h: (b, h))
    vec = pl.BlockSpec((1, HEAD_DIM // 2), lambda b, h: (0, 0))
    return pl.pallas_call(
        functools.partial(_diff_kernel, lam_init=lam_init),
        grid=(batch, DIFF_HEADS),
        in_specs=[spec, spec, spec, vec, vec, vec, vec, pl.BlockSpec((1, HEAD_DIM), lambda b, h: (0, h))],
        out_specs=spec,
        out_shape=jax.ShapeDtypeStruct(dq.shape, BF16),
        compiler_params=pltpu.CompilerParams(
            dimension_semantics=("arbitrary", "arbitrary"),
            vmem_limit_bytes=_vmem_limit(4 * _nbytes((seq, HEAD_DIM), BF16),
                                         8 * _nbytes((DIFF_QBLOCK, seq), F32)),
        ),
    )(dq, dk, dv, lq1, lk1, lq2, lk2, subln_g)


def _outproj_kernel(r_ref, d_ref, x_ref, w_ref, g_ref, b_ref, xt_ref, xtb_ref, *, alpha):
    mix = (jnp.dot(r_ref[...], w_ref[0:GROUP, :], preferred_element_type=F32)
           + jnp.dot(d_ref[...], w_ref[GROUP:2 * GROUP, :], preferred_element_type=F32))
    y = alpha * x_ref[...] + mix
    mu = jnp.mean(y, axis=-1, keepdims=True)
    cen = y - mu
    var = jnp.mean(cen * cen, axis=-1, keepdims=True)
    ln = cen * lax.rsqrt(var + LN_EPS) * g_ref[...] + b_ref[...]
    ln_t = ln.T
    xt_ref[...] = ln_t
    xtb_ref[...] = ln_t.astype(xtb_ref.dtype)


def _out_projection(r, d, x2d, w_out, ln_g, ln_b, alpha):
    tokens, d_model = x2d.shape
    tm = OUTPROJ_ROWS
    half = pl.BlockSpec((tm, GROUP), lambda i: (i, 0))
    par = pl.BlockSpec((1, d_model), lambda i: (0, 0))
    t_spec = pl.BlockSpec((d_model, tm), lambda i: (0, i))
    return pl.pallas_call(
        functools.partial(_outproj_kernel, alpha=alpha),
        grid=(tokens // tm,),
        in_specs=[half, half, pl.BlockSpec((tm, d_model), lambda i: (i, 0)),
                  pl.BlockSpec((2 * GROUP, d_model), lambda i: (0, 0)), par, par],
        out_specs=[t_spec, t_spec],
        out_shape=[jax.ShapeDtypeStruct((d_model, tokens), F32), jax.ShapeDtypeStruct((d_model, tokens), BF16)],
        compiler_params=pltpu.CompilerParams(
            dimension_semantics=("arbitrary",),
            vmem_limit_bytes=_vmem_limit(
                2 * _nbytes((tm, GROUP), BF16) + _nbytes((tm, d_model), F32) + _nbytes((2 * GROUP, d_model), BF16)
                + _nbytes((d_model, tm), F32) + _nbytes((d_model, tm), BF16),
                4 * _nbytes((tm, d_model), F32)),
        ),
    )(r, d, x2d, w_out, ln_g, ln_b)


def _candidate_cells():
    return [(p, q) for p in range(PEER_TOPK) for q in range(PEER_TOPK) if (p + 1) * (q + 1) <= PEER_TOPK]


def _route_kernel(xt_ref, wq_ref, keys_ref, rank2_ref, e2_ref, lrow_ref, e1_ref,
                  q_scr, score_scr, rank_scr, top_scr, len_scr, invz_scr):
    tt = xt_ref.shape[1]
    k = PEER_TOPK
    q_scr[...] = jnp.dot(wq_ref[...], xt_ref[...], preferred_element_type=F32).astype(BF16)
    key_iota = lax.broadcasted_iota(jnp.int32, (PEER_KEYS, V7X_LANES), 0).astype(F32)

    for slot in range(2 * PEER_HEADS):
        src = (slot % PEER_HEADS) * 2 + slot // PEER_HEADS
        scores = jnp.dot(keys_ref[src], q_scr[src * PEER_KEYS:(src + 1) * PEER_KEYS, :],
                         preferred_element_type=F32)
        score_scr[slot] = scores
        for lb in range(tt // V7X_LANES):
            ls = slice(lb * V7X_LANES, (lb + 1) * V7X_LANES)

            def extract(r, state, slot=slot, ls=ls):
                s, rank = state
                m = jnp.max(s, axis=0, keepdims=True)
                first = jnp.min(jnp.where(s == m, key_iota, float(PEER_KEYS)), axis=0, keepdims=True)
                hit = key_iota == first
                top_scr[r, slot:slot + 1, ls] = m
                return jnp.where(hit, -jnp.inf, s), jnp.where(hit, r.astype(F32), rank)

            _, rank = lax.fori_loop(0, k, extract, (scores[:, ls], jnp.full((PEER_KEYS, V7X_LANES), float(k), F32)))
            rank_scr[slot, :, ls] = rank

    cells = _candidate_cells()
    a = [top_scr[p, 0:PEER_HEADS, :] for p in range(k)]
    b = [top_scr[q, PEER_HEADS:2 * PEER_HEADS, :] for q in range(k)]
    total = {cell: a[cell[0]] + b[cell[1]] for cell in cells}
    ahead = {cell: jnp.full((PEER_HEADS, tt), float((cell[0] + 1) * (cell[1] + 1) - 1), F32) for cell in cells}
    for xi, x in enumerate(cells):
        for y in cells[xi + 1:]:
            if (x[0] <= y[0] and x[1] <= y[1]) or (x[0] >= y[0] and x[1] >= y[1]):
                continue
            x_first = total[x] >= total[y]
            ahead[y] = ahead[y] + jnp.where(x_first, 1.0, 0.0)
            ahead[x] = ahead[x] + jnp.where(x_first, 0.0, 1.0)
    best = total[(0, 0)]
    denom = jnp.zeros((PEER_HEADS, tt), F32)
    counts = [jnp.zeros((PEER_HEADS, tt), F32) for _ in range(k)]
    for cell in cells:
        chosen = ahead[cell] < float(k)
        denom = denom + jnp.where(chosen, jnp.exp(total[cell] - best), 0.0)
        counts[cell[0]] = counts[cell[0]] + jnp.where(chosen, 1.0, 0.0)
    invz_scr[...] = 1.0 / denom
    for p in range(k):
        len_scr[p] = counts[p]

    for head in range(PEER_HEADS):
        hs = slice(head, head + 1)
        other = PEER_HEADS + head
        rank1 = rank_scr[head]
        lrow = jnp.zeros((PEER_KEYS, tt), F32)
        for p in range(k):
            lrow = lrow + jnp.where(rank1 == float(p), len_scr[p, hs, :], 0.0)
        lrow_ref[head] = lrow
        e1_ref[head] = jnp.exp(score_scr[head] - top_scr[0, hs, :]) * invz_scr[hs, :]
        rank2_ref[head] = rank_scr[other]
        e2_ref[head] = jnp.exp(score_scr[other] - top_scr[0, other:other + 1, :])


def _peer_route(xt_bf16, wq_t, keys):
    d_model, tokens = xt_bf16.shape
    tt = ROUTE_TOKENS
    qdim = wq_t.shape[0]
    slots = 2 * PEER_HEADS
    out = jax.ShapeDtypeStruct((PEER_HEADS, PEER_KEYS, tokens), F32)
    out_spec = pl.BlockSpec((PEER_HEADS, PEER_KEYS, tt), lambda t: (0, 0, t))
    return pl.pallas_call(
        _route_kernel,
        grid=(tokens // tt,),
        in_specs=[pl.BlockSpec((d_model, tt), lambda t: (0, t)),
                  pl.BlockSpec((qdim, d_model), lambda t: (0, 0)),
                  pl.BlockSpec((slots, PEER_KEYS, PEER_KEYS), lambda t: (0, 0, 0))],
        out_specs=[out_spec] * 4,
        out_shape=[out] * 4,
        scratch_shapes=[
            pltpu.VMEM((qdim, tt), BF16),
            pltpu.VMEM((slots, PEER_KEYS, tt), F32),
            pltpu.VMEM((slots, PEER_KEYS, tt), F32),
            pltpu.VMEM((PEER_TOPK, slots, tt), F32),
            pltpu.VMEM((PEER_TOPK, PEER_HEADS, tt), F32),
            pltpu.VMEM((PEER_HEADS, tt), F32),
        ],
        compiler_params=pltpu.CompilerParams(
            dimension_semantics=("arbitrary",),
            vmem_limit_bytes=_vmem_limit(
                _nbytes((d_model, tt), BF16) + _nbytes((qdim, d_model), BF16)
                + _nbytes((slots, PEER_KEYS, PEER_KEYS), BF16) + 4 * _nbytes((PEER_HEADS, PEER_KEYS, tt), F32),
                _nbytes((qdim, tt), F32) + 3 * _nbytes((slots, PEER_KEYS, tt), F32)),
        ),
    )(xt_bf16, wq_t, keys)


def _expert_kernel(xtb_ref, xt_ref, u_ref, vt_ref, rank2_ref, e2_ref, lrow_ref, e1_ref, g_ref, b_ref,
                   o_ref, acc_ref, h_scr, w_scr, *, alpha):
    step = pl.program_id(1)

    @pl.when(step == 0)
    def _():
        acc_ref[...] = jnp.zeros_like(acc_ref)

    h_scr[...] = jnp.dot(u_ref[...], xtb_ref[...], preferred_element_type=F32)
    rows_per_key = PEER_KEYS // GATE_ROWS
    sqrt_half = float(np.float32(np.sqrt(0.5)))

    def gate_rows(it, carry):
        i_local = it // rows_per_key
        j0 = pl.multiple_of((it % rows_per_key) * GATE_ROWS, GATE_ROWS)
        n0 = pl.multiple_of(it * GATE_ROWS, GATE_ROWS)
        gate = jnp.zeros((GATE_ROWS, h_scr.shape[1]), F32)
        for head in range(PEER_HEADS):
            lrow = lrow_ref[head, pl.ds(i_local, 1), :]
            e1 = e1_ref[head, pl.ds(i_local, 1), :]
            rank2 = rank2_ref[head, pl.ds(j0, GATE_ROWS), :]
            e2 = e2_ref[head, pl.ds(j0, GATE_ROWS), :]
            gate = gate + jnp.where(rank2 < lrow, e2, 0.0) * e1
        hid = h_scr[pl.ds(n0, GATE_ROWS), :]
        gelu = 0.5 * hid * (1.0 + lax.erf(hid * sqrt_half))
        w_scr[pl.ds(n0, GATE_ROWS), :] = (gate * gelu).astype(w_scr.dtype)
        return carry

    lax.fori_loop(0, h_scr.shape[0] // GATE_ROWS, gate_rows, 0)
    acc_ref[...] += jnp.dot(vt_ref[...], w_scr[...], preferred_element_type=F32)

    @pl.when(step == pl.num_programs(1) - 1)
    def _():
        y = alpha * xt_ref[...] + acc_ref[...]
        mu = jnp.mean(y, axis=0, keepdims=True)
        cen = y - mu
        var = jnp.mean(cen * cen, axis=0, keepdims=True)
        ln = cen * lax.rsqrt(var + LN_EPS) * g_ref[...] + b_ref[...]
        o_ref[...] = ln.T.astype(o_ref.dtype)


def _peer_experts(xt_bf16, xt_f32, u, v_t, rank2, e2, lrow, e1, ln_g, ln_b, alpha, out_dtype):
    d_model, tokens = xt_f32.shape
    experts = u.shape[0]
    tt = EXPERT_TOKENS
    eb = EXPERT_BLOCK
    keys_per_block = eb // PEER_KEYS
    full = pl.BlockSpec((PEER_HEADS, PEER_KEYS, tt), lambda t, e: (0, 0, t))
    rows = pl.BlockSpec((PEER_HEADS, keys_per_block, tt), lambda t, e: (0, e, t))
    col = pl.BlockSpec((d_model, 1), lambda t, e: (0, 0))
    return pl.pallas_call(
        functools.partial(_expert_kernel, alpha=alpha),
        grid=(tokens // tt, experts // eb),
        in_specs=[pl.BlockSpec((d_model, tt), lambda t, e: (0, t)),
                  pl.BlockSpec((d_model, tt), lambda t, e: (0, t)),
                  pl.BlockSpec((eb, d_model), lambda t, e: (e, 0)),
                  pl.BlockSpec((d_model, eb), lambda t, e: (0, e)),
                  full, full, rows, rows, col, col],
        out_specs=pl.BlockSpec((tt, d_model), lambda t, e: (t, 0)),
        out_shape=jax.ShapeDtypeStruct((tokens, d_model), out_dtype),
        scratch_shapes=[pltpu.VMEM((d_model, tt), F32), pltpu.VMEM((eb, tt), F32), pltpu.VMEM((eb, tt), BF16)],
        compiler_params=pltpu.CompilerParams(
            dimension_semantics=("arbitrary", "arbitrary"),
            vmem_limit_bytes=_vmem_limit(
                _nbytes((d_model, tt), BF16) + _nbytes((d_model, tt), F32) + 2 * _nbytes((eb, d_model), BF16)
                + 2 * _nbytes((PEER_HEADS, PEER_KEYS, tt), F32) + 2 * _nbytes((PEER_HEADS, keys_per_block, tt), F32)
                + 2 * _nbytes((d_model, V7X_LANES), F32) + _nbytes((tt, d_model), F32),
                _nbytes((d_model, tt), F32) + _nbytes((eb, tt), F32) + _nbytes((eb, tt), BF16)
                + 4 * _nbytes((d_model, tt), F32)),
        ),
    )(xt_bf16, xt_f32, u, v_t, rank2, e2, lrow, e1, ln_g, ln_b)


def _lambda_init(layer_idx):
    return 0.8 - 0.6 * math.exp(-0.3 * (layer_idx - 1))


def kernel(x, w_in, ret_gn_g, ret_gn_b, diff_lambda_q1, diff_lambda_k1, diff_lambda_q2, diff_lambda_k2,
           diff_subln_g, w_out, ln1_g, ln1_b, peer_w_query, peer_sub_keys, peer_u, peer_v, ln2_g, ln2_b):
    batch, seq, d_model = x.shape
    depth = w_in.shape[0]
    tokens = batch * seq
    alpha = (2.0 * depth) ** 0.25
    assert w_in.shape[2] == 7 * GROUP and d_model == 2 * GROUP
    assert seq % max(INPROJ_ROWS, RET_ROWS, DIFF_QBLOCK) == 0
    assert tokens % max(OUTPROJ_ROWS, ROUTE_TOKENS, EXPERT_TOKENS) == 0
    assert peer_u.shape[1] == PEER_KEYS * PEER_KEYS and peer_u.shape[1] % EXPERT_BLOCK == 0
    assert peer_sub_keys.shape[1:] == (PEER_HEADS, 2, PEER_KEYS, PEER_KEYS)

    cos_tab, sin_tab = _rope_tables(seq)
    x2d = x.reshape(tokens, d_model)
    for l in range(depth):
        lam_init = _lambda_init(l + 1)
        row = lambda p: p[l].reshape(1, -1)
        rq, rk, rv, rg, dq, dk, dv = _in_projection(x2d, w_in[l].astype(BF16), cos_tab, sin_tab, seq)
        r = _retention(rq, rk, rv, rg, row(ret_gn_g), row(ret_gn_b), batch, seq)
        d = _diff_attention(dq, dk, dv, row(diff_lambda_q1), row(diff_lambda_k1), row(diff_lambda_q2),
                            row(diff_lambda_k2), row(diff_subln_g), batch, seq, lam_init)
        xt_f32, xt_bf16 = _out_projection(r, d, x2d, w_out[l].astype(BF16), row(ln1_g), row(ln1_b), alpha)
        keys = peer_sub_keys[l].reshape(2 * PEER_HEADS, PEER_KEYS, PEER_KEYS).astype(BF16)
        rank2, e2, lrow, e1 = _peer_route(xt_bf16, peer_w_query[l].T.astype(BF16), keys)
        x2d = _peer_experts(xt_bf16, xt_f32, peer_u[l].astype(BF16), peer_v[l].T.astype(BF16),
                            rank2, e2, lrow, e1, ln2_g[l].reshape(-1, 1), ln2_b[l].reshape(-1, 1), alpha, x.dtype)
    return x2d.reshape(batch, seq, d_model)
```

```python
import functools
import math

import jax
import jax.numpy as jnp
import numpy as np
from jax import lax
from jax.experimental import pallas as pl
from jax.experimental.pallas import tpu as pltpu

F32 = jnp.float32
BF16 = jnp.bfloat16

V7X_LANES = 128
V7X_SUBLANES = 8
V7X_VMEM_BYTES = 64 * 1024 * 1024

RET_HEADS = 4
DIFF_HEADS = 4
HEAD_DIM = 128
RET_CHUNK = 128
ROPE_BASE = 10000.0
PEER_HEADS = 8
PEER_KEYS = 128
PEER_TOPK = 16
LN_EPS = 1e-5
GROUP = RET_HEADS * HEAD_DIM

INPROJ_ROWS = 512
RET_ROWS = 512
DIFF_QBLOCK = 256
OUTPROJ_ROWS = 512
ROUTE_TOKENS = 256
ROUTE_SLOTS_PER_LOOP = 2
EXPERT_TOKENS = 512
EXPERT_BLOCK = 1024
EXPERT_SUBBLOCK = 256


def _vmem_limit(pipelined_bytes, resident_bytes):
    need = 2 * pipelined_bytes + resident_bytes
    return int(min(max(need, 16 * 1024 * 1024), V7X_VMEM_BYTES - 8 * 1024 * 1024))


def _nbytes(shape, dtype):
    return int(np.prod(shape)) * jnp.dtype(dtype).itemsize


def _rope_kernel(cos_ref, sin_ref):
    shape = cos_ref.shape
    half = HEAD_DIM // 2
    pos = lax.broadcasted_iota(jnp.int32, shape, 0).astype(F32)
    lane = lax.broadcasted_iota(jnp.int32, shape, 1)
    freq_idx = jnp.where(lane >= half, lane - half, lane).astype(F32)
    inv = jnp.exp(freq_idx * (-math.log(ROPE_BASE) / half))
    ang = pos * inv
    cos_ref[...] = jnp.cos(ang)
    sin = jnp.sin(ang)
    sin_ref[...] = jnp.where(lane >= half, sin, -sin)


def _rope_tables(seq):
    return pl.pallas_call(
        _rope_kernel,
        out_shape=(jax.ShapeDtypeStruct((seq, HEAD_DIM), F32),) * 2,
    )()


def _inproj_kernel(x_ref, w_ref, cos_ref, sin_ref, rq_ref, rk_ref, rv_ref, rg_ref, dq_ref, dk_ref, dv_ref):
    xb = x_ref[...].astype(BF16)
    cos = cos_ref[...]
    sin = sin_ref[...]

    def proj(group):
        return jnp.dot(xb, w_ref[:, group * GROUP:(group + 1) * GROUP], preferred_element_type=F32)

    def store_rotary(out_ref, p, scale):
        for h in range(RET_HEADS):
            ph = p[:, h * HEAD_DIM:(h + 1) * HEAD_DIM]
            rot = ph * cos + pltpu.roll(ph, HEAD_DIM // 2, axis=1) * sin
            if scale != 1.0:
                rot = rot * scale
            out_ref[:, h * HEAD_DIM:(h + 1) * HEAD_DIM] = rot.astype(out_ref.dtype)

    store_rotary(rq_ref, proj(0), 1.0)
    store_rotary(rk_ref, proj(1), HEAD_DIM ** -0.5)
    rv_ref[...] = proj(2).astype(rv_ref.dtype)
    rg_ref[...] = proj(3).astype(rg_ref.dtype)
    dq_ref[...] = (proj(4) * ((HEAD_DIM // 2) ** -0.5)).astype(dq_ref.dtype)
    dk_ref[...] = proj(5).astype(dk_ref.dtype)
    dv_ref[...] = proj(6).astype(dv_ref.dtype)


def _in_projection(x2d, w_in, cos_tab, sin_tab, seq):
    tokens, d_model = x2d.shape
    cols = w_in.shape[1]
    tm = INPROJ_ROWS
    pos_blocks = seq // tm
    out = jax.ShapeDtypeStruct((tokens, GROUP), BF16)
    group_spec = pl.BlockSpec((tm, GROUP), lambda i: (i, 0))
    tab_spec = pl.BlockSpec((tm, HEAD_DIM), lambda i: (i % pos_blocks, 0))
    return pl.pallas_call(
        _inproj_kernel,
        grid=(tokens // tm,),
        in_specs=[
            pl.BlockSpec((tm, d_model), lambda i: (i, 0)),
            pl.BlockSpec((d_model, cols), lambda i: (0, 0)),
            tab_spec,
            tab_spec,
        ],
        out_specs=[group_spec] * 7,
        out_shape=[out] * 7,
        compiler_params=pltpu.CompilerParams(
            dimension_semantics=("arbitrary",),
            vmem_limit_bytes=_vmem_limit(
                _nbytes((tm, d_model), F32) + _nbytes((d_model, cols), BF16) + 7 * _nbytes((tm, GROUP), BF16)
                + 2 * _nbytes((tm, HEAD_DIM), F32),
                _nbytes((tm, d_model), BF16) + 3 * _nbytes((tm, GROUP), F32)),
        ),
    )(x2d, w_in, cos_tab, sin_tab)


def _retention_kernel(q_ref, k_ref, v_ref, g_ref, gn_g_ref, gn_b_ref, o_ref, state_ref):
    @pl.when(pl.program_id(1) == 0)
    def _():
        state_ref[...] = jnp.zeros_like(state_ref)

    c = RET_CHUNK
    row = lax.broadcasted_iota(jnp.int32, (c, c), 0).astype(F32)
    col = lax.broadcasted_iota(jnp.int32, (c, c), 1).astype(F32)
    rel = row - col
    idx = lax.broadcasted_iota(jnp.int32, (c, 1), 0).astype(F32)
    for h in range(RET_HEADS):
        log_g = math.log1p(-(2.0 ** (-5.0 - h)))
        decay_in = jnp.where(rel >= 0, jnp.exp(jnp.maximum(rel, 0.0) * log_g), 0.0)
        decay_q = jnp.exp((idx + 1.0) * log_g)
        decay_k = jnp.exp((c - 1.0 - idx) * log_g)
        decay_c = math.exp(c * log_g)
        hs = slice(h * HEAD_DIM, (h + 1) * HEAD_DIM)
        gn_g = gn_g_ref[:, hs]
        gn_b = gn_b_ref[:, hs]
        state = state_ref[h]
        for ci in range(q_ref.shape[0] // c):
            rs = slice(ci * c, (ci + 1) * c)
            q = q_ref[rs, hs]
            k = k_ref[rs, hs]
            v = v_ref[rs, hs]
            s = lax.dot_general(q, k, (((1,), (1,)), ((), ())), preferred_element_type=F32) * decay_in
            inner = jnp.dot(s.astype(BF16), v, preferred_element_type=F32)
            cross = jnp.dot(q, state.astype(BF16), preferred_element_type=F32) * decay_q
            kd_t = (k.astype(F32) * decay_k).T.astype(BF16)
            state = state * decay_c + jnp.dot(kd_t, v, preferred_element_type=F32)
            o = inner + cross
            mu = jnp.mean(o, axis=-1, keepdims=True)
            cen = o - mu
            var = jnp.mean(cen * cen, axis=-1, keepdims=True)
            normed = cen * lax.rsqrt(var + LN_EPS) * gn_g + gn_b
            gate = g_ref[rs, hs].astype(F32)
            o_ref[rs, hs] = (gate * jax.nn.sigmoid(gate) * normed).astype(o_ref.dtype)
        state_ref[h] = state


def _retention(rq, rk, rv, rg, gn_g, gn_b, batch, seq):
    rows = RET_ROWS
    blocks = seq // rows
    spec = pl.BlockSpec((rows, GROUP), lambda b, c: (b * blocks + c, 0))
    par = pl.BlockSpec((1, GROUP), lambda b, c: (0, 0))
    return pl.pallas_call(
        _retention_kernel,
        grid=(batch, blocks),
        in_specs=[spec, spec, spec, spec, par, par],
        out_specs=spec,
        out_shape=jax.ShapeDtypeStruct(rq.shape, BF16),
        scratch_shapes=[pltpu.VMEM((RET_HEADS, HEAD_DIM, HEAD_DIM), F32)],
        compiler_params=pltpu.CompilerParams(
            dimension_semantics=("arbitrary", "arbitrary"),
            vmem_limit_bytes=_vmem_limit(5 * _nbytes((rows, GROUP), BF16), 4 * 1024 * 1024),
        ),
    )(rq, rk, rv, rg, gn_g, gn_b)


def _diff_kernel(q_ref, k_ref, v_ref, lq1_ref, lk1_ref, lq2_ref, lk2_ref, g_ref, o_ref, *, lam_init):
    f = lambda r: r[...].astype(F32)
    lam = (jnp.exp(jnp.sum(f(lq1_ref) * f(lk1_ref), axis=-1, keepdims=True))
           - jnp.exp(jnp.sum(f(lq2_ref) * f(lk2_ref), axis=-1, keepdims=True)) + lam_init)
    seq = q_ref.shape[0]
    bq = DIFF_QBLOCK
    half = HEAD_DIM // 2
    lane = lax.broadcasted_iota(jnp.int32, (bq, HEAD_DIM), 1)
    tri = (lax.broadcasted_iota(jnp.int32, (bq, bq), 1) <= lax.broadcasted_iota(jnp.int32, (bq, bq), 0))
    nt = (((1,), (1,)), ((), ()))
    zero = jnp.zeros((), BF16)
    for qi in range(seq // bq):
        past = qi * bq
        q = q_ref[past:past + bq, :]
        q_comp = (jnp.where(lane < half, q, zero), jnp.where(lane >= half, q, zero))
        k_diag = k_ref[past:past + bq, :]
        attn_diag = None
        attn_past = None
        for comp in range(2):
            s_diag = lax.dot_general(q_comp[comp], k_diag, nt, preferred_element_type=F32)
            s_diag = jnp.where(tri, s_diag, -jnp.inf)
            m = jnp.max(s_diag, axis=-1, keepdims=True)
            if past:
                s_past = lax.dot_general(q_comp[comp], k_ref[0:past, :], nt, preferred_element_type=F32)
                m = jnp.maximum(m, jnp.max(s_past, axis=-1, keepdims=True))
                e_past = jnp.exp(s_past - m)
            e_diag = jnp.exp(s_diag - m)
            denom = jnp.sum(e_diag, axis=-1, keepdims=True)
            if past:
                denom = denom + jnp.sum(e_past, axis=-1, keepdims=True)
            weight = (1.0 / denom) if comp == 0 else (-lam / denom)
            attn_diag = e_diag * weight if comp == 0 else attn_diag + e_diag * weight
            if past:
                attn_past = e_past * weight if comp == 0 else attn_past + e_past * weight
        o = jnp.dot(attn_diag.astype(BF16), v_ref[past:past + bq, :], preferred_element_type=F32)
        if past:
            o = o + jnp.dot(attn_past.astype(BF16), v_ref[0:past, :], preferred_element_type=F32)
        normed = o * lax.rsqrt(jnp.mean(o * o, axis=-1, keepdims=True) + LN_EPS)
        o_ref[past:past + bq, :] = (normed * g_ref[...] * (1.0 - lam_init)).astype(o_ref.dtype)


def _diff_attention(dq, dk, dv, lq1, lk1, lq2, lk2, subln_g, batch, seq, lam_init):
    spec = pl.BlockSpec((seq, HEAD_DIM), lambda b, h: (b, h))
    vec = pl.BlockSpec((1, HEAD_DIM // 2), lambda b, h: (0, 0))
    return pl.pallas_call(
        functools.partial(_diff_kernel, lam_init=lam_init),
        grid=(batch, DIFF_HEADS),
        in_specs=[spec, spec, spec, vec, vec, vec, vec, pl.BlockSpec((1, HEAD_DIM), lambda b, h: (0, h))],
        out_specs=spec,
        out_shape=jax.ShapeDtypeStruct(dq.shape, BF16),
        compiler_params=pltpu.CompilerParams(
            dimension_semantics=("arbitrary", "arbitrary"),
            vmem_limit_bytes=_vmem_limit(4 * _nbytes((seq, HEAD_DIM), BF16),
                                         8 * _nbytes((DIFF_QBLOCK, seq), F32)),
        ),
    )(dq, dk, dv, lq1, lk1, lq2, lk2, subln_g)


def _outproj_kernel(r_ref, d_ref, x_ref, w_ref, g_ref, b_ref, xt_ref, xtb_ref, *, alpha):
    mix = (jnp.dot(r_ref[...], w_ref[0:GROUP, :], preferred_element_type=F32)
           + jnp.dot(d_ref[...], w_ref[GROUP:2 * GROUP, :], preferred_element_type=F32))
    y = alpha * x_ref[...] + mix
    mu = jnp.mean(y, axis=-1, keepdims=True)
    cen = y - mu
    var = jnp.mean(cen * cen, axis=-1, keepdims=True)
    ln = cen * lax.rsqrt(var + LN_EPS) * g_ref[...] + b_ref[...]
    ln_t = ln.T
    xt_ref[...] = ln_t
    xtb_ref[...] = ln_t.astype(xtb_ref.dtype)


def _out_projection(r, d, x2d, w_out, ln_g, ln_b, alpha):
    tokens, d_model = x2d.shape
    tm = OUTPROJ_ROWS
    half = pl.BlockSpec((tm, GROUP), lambda i: (i, 0))
    par = pl.BlockSpec((1, d_model), lambda i: (0, 0))
    t_spec = pl.BlockSpec((d_model, tm), lambda i: (0, i))
    return pl.pallas_call(
        functools.partial(_outproj_kernel, alpha=alpha),
        grid=(tokens // tm,),
        in_specs=[half, half, pl.BlockSpec((tm, d_model), lambda i: (i, 0)),
                  pl.BlockSpec((2 * GROUP, d_model), lambda i: (0, 0)), par, par],
        out_specs=[t_spec, t_spec],
        out_shape=[jax.ShapeDtypeStruct((d_model, tokens), F32), jax.ShapeDtypeStruct((d_model, tokens), BF16)],
        compiler_params=pltpu.CompilerParams(
            dimension_semantics=("arbitrary",),
            vmem_limit_bytes=_vmem_limit(
                2 * _nbytes((tm, GROUP), BF16) + _nbytes((tm, d_model), F32) + _nbytes((2 * GROUP, d_model), BF16)
                + _nbytes((d_model, tm), F32) + _nbytes((d_model, tm), BF16),
                4 * _nbytes((tm, d_model), F32)),
        ),
    )(r, d, x2d, w_out, ln_g, ln_b)


def _candidate_cells():
    return [(p, q) for p in range(PEER_TOPK) for q in range(PEER_TOPK) if (p + 1) * (q + 1) <= PEER_TOPK]


def _route_kernel(xt_ref, wq_ref, keys_ref, rank2_ref, e2_ref, lrow_ref, e1_ref,
                  q_scr, score_scr, top_scr, idx_scr, len_scr, invz_scr):
    tt = xt_ref.shape[1]
    k = PEER_TOPK
    sub = V7X_SUBLANES
    groups = PEER_KEYS // sub
    lane_tiles = tt // V7X_LANES
    q_scr[...] = jnp.dot(wq_ref[...], xt_ref[...], preferred_element_type=F32).astype(BF16)
    group_iota = [(lax.broadcasted_iota(jnp.int32, (sub, V7X_LANES), 0) + g * sub).astype(F32) for g in range(groups)]

    for slot in range(2 * PEER_HEADS):
        src = (slot % PEER_HEADS) * 2 + slot // PEER_HEADS
        score_scr[slot] = jnp.dot(keys_ref[src], q_scr[src * PEER_KEYS:(src + 1) * PEER_KEYS, :],
                                  preferred_element_type=F32)

    for first_slot in range(0, 2 * PEER_HEADS, ROUTE_SLOTS_PER_LOOP):
        chains = [(slot, lt) for slot in range(first_slot, first_slot + ROUTE_SLOTS_PER_LOOP)
                  for lt in range(lane_tiles)]

        def extract(r, state, chains=chains):
            out = []
            for c, (slot, lt) in enumerate(chains):
                ls = slice(lt * V7X_LANES, (lt + 1) * V7X_LANES)
                cur = state[c * groups:(c + 1) * groups]
                vals, idxs = list(cur), list(group_iota)
                while len(vals) > 1:
                    left = [vals[a] >= vals[a + 1] for a in range(0, len(vals), 2)]
                    idxs = [jnp.where(left[a // 2], idxs[a], idxs[a + 1]) for a in range(0, len(vals), 2)]
                    vals = [jnp.maximum(vals[a], vals[a + 1]) for a in range(0, len(vals), 2)]
                v, i = vals[0], idxs[0]
                for shift in (4, 2, 1):
                    v_other = pltpu.roll(v, shift, axis=0)
                    i_other = pltpu.roll(i, shift, axis=0)
                    keep = (v > v_other) | ((v == v_other) & (i < i_other))
                    v = jnp.where(keep, v, v_other)
                    i = jnp.where(keep, i, i_other)
                top_scr[r, slot:slot + 1, ls] = v[0:1, :]
                idx_scr[r, slot:slot + 1, ls] = i[0:1, :]
                out.extend(jnp.where(group_iota[g] == i, -jnp.inf, cur[g]) for g in range(groups))
            return tuple(out)

        init = tuple(score_scr[slot, g * sub:(g + 1) * sub, lt * V7X_LANES:(lt + 1) * V7X_LANES]
                     for slot, lt in chains for g in range(groups))
        lax.fori_loop(0, k, extract, init)

    cells = _candidate_cells()
    a = [top_scr[p, 0:PEER_HEADS, :] for p in range(k)]
    b = [top_scr[q, PEER_HEADS:2 * PEER_HEADS, :] for q in range(k)]
    total = {cell: a[cell[0]] + b[cell[1]] for cell in cells}
    ahead = {cell: jnp.full((PEER_HEADS, tt), float((cell[0] + 1) * (cell[1] + 1) - 1), F32) for cell in cells}
    for xi, x in enumerate(cells):
        for y in cells[xi + 1:]:
            if (x[0] <= y[0] and x[1] <= y[1]) or (x[0] >= y[0] and x[1] >= y[1]):
                continue
            x_first = total[x] >= total[y]
            ahead[y] = ahead[y] + jnp.where(x_first, 1.0, 0.0)
            ahead[x] = ahead[x] + jnp.where(x_first, 0.0, 1.0)
    best = total[(0, 0)]
    denom = jnp.zeros((PEER_HEADS, tt), F32)
    counts = [jnp.zeros((PEER_HEADS, tt), F32) for _ in range(k)]
    for cell in cells:
        chosen = ahead[cell] < float(k)
        denom = denom + jnp.where(chosen, jnp.exp(total[cell] - best), 0.0)
        counts[cell[0]] = counts[cell[0]] + jnp.where(chosen, 1.0, 0.0)
    invz_scr[...] = 1.0 / denom
    for p in range(k):
        len_scr[p] = counts[p]

    key_iota = lax.broadcasted_iota(jnp.int32, (PEER_KEYS, tt), 0).astype(F32)
    for head in range(PEER_HEADS):
        hs = slice(head, head + 1)
        other = PEER_HEADS + head
        os_ = slice(other, other + 1)
        lrow = jnp.zeros((PEER_KEYS, tt), F32)
        rank2 = jnp.full((PEER_KEYS, tt), float(k), F32)
        for r in range(k):
            lrow = jnp.where(key_iota == idx_scr[r, hs, :], len_scr[r, hs, :], lrow)
            rank2 = jnp.where(key_iota == idx_scr[r, os_, :], float(r), rank2)
        lrow_ref[head] = lrow
        rank2_ref[head] = rank2.astype(rank2_ref.dtype)
        e1_ref[head] = jnp.exp(score_scr[head] - top_scr[0, hs, :]) * invz_scr[hs, :]
        e2_ref[head] = jnp.exp(score_scr[other] - top_scr[0, os_, :]).astype(e2_ref.dtype)


def _peer_route(xt_bf16, wq_t, keys):
    d_model, tokens = xt_bf16.shape
    tt = ROUTE_TOKENS
    qdim = wq_t.shape[0]
    slots = 2 * PEER_HEADS
    shape = (PEER_HEADS, PEER_KEYS, tokens)
    out_spec = pl.BlockSpec((PEER_HEADS, PEER_KEYS, tt), lambda t: (0, 0, t))
    return pl.pallas_call(
        _route_kernel,
        grid=(tokens // tt,),
        in_specs=[pl.BlockSpec((d_model, tt), lambda t: (0, t)),
                  pl.BlockSpec((qdim, d_model), lambda t: (0, 0)),
                  pl.BlockSpec((slots, PEER_KEYS, PEER_KEYS), lambda t: (0, 0, 0))],
        out_specs=[out_spec] * 4,
        out_shape=[jax.ShapeDtypeStruct(shape, BF16), jax.ShapeDtypeStruct(shape, BF16),
                   jax.ShapeDtypeStruct(shape, F32), jax.ShapeDtypeStruct(shape, F32)],
        scratch_shapes=[
            pltpu.VMEM((qdim, tt), BF16),
            pltpu.VMEM((slots, PEER_KEYS, tt), F32),
            pltpu.VMEM((PEER_TOPK, slots, tt), F32),
            pltpu.VMEM((PEER_TOPK, slots, tt), F32),
            pltpu.VMEM((PEER_TOPK, PEER_HEADS, tt), F32),
            pltpu.VMEM((PEER_HEADS, tt), F32),
        ],
        compiler_params=pltpu.CompilerParams(
            dimension_semantics=("arbitrary",),
            vmem_limit_bytes=_vmem_limit(
                _nbytes((d_model, tt), BF16) + _nbytes((qdim, d_model), BF16)
                + _nbytes((slots, PEER_KEYS, PEER_KEYS), BF16) + 3 * _nbytes((PEER_HEADS, PEER_KEYS, tt), F32),
                _nbytes((qdim, tt), F32) + 4 * _nbytes((slots, PEER_KEYS, tt), F32)),
        ),
    )(xt_bf16, wq_t, keys)


def _expert_kernel(xtb_ref, xt_ref, u_ref, vt_ref, rank2_ref, e2_ref, lrow_ref, e1_ref, g_ref, b_ref,
                   o_ref, acc_ref, h_scr, w_scr, rank2_scr, e2_scr, *, alpha):
    step = pl.program_id(1)

    @pl.when(step == 0)
    def _():
        acc_ref[...] = jnp.zeros_like(acc_ref)
        rank2_scr[...] = rank2_ref[...]
        e2_scr[...] = e2_ref[...]

    eb, tt = h_scr.shape
    packed = 2 * V7X_SUBLANES
    keys_per_sub = EXPERT_SUBBLOCK // PEER_KEYS
    sqrt_half = float(np.float32(np.sqrt(0.5)))
    zero = jnp.zeros((), BF16)
    sub_blocks = eb // EXPERT_SUBBLOCK

    def hidden(sb):
        rows = slice(sb * EXPERT_SUBBLOCK, (sb + 1) * EXPERT_SUBBLOCK)
        h_scr[rows, :] = jnp.dot(u_ref[rows, :], xtb_ref[...], preferred_element_type=F32)

    hidden(0)
    for sb in range(sub_blocks):
        rows = slice(sb * EXPERT_SUBBLOCK, (sb + 1) * EXPERT_SUBBLOCK)
        if sb + 1 < sub_blocks:
            hidden(sb + 1)
        for il in range(sb * keys_per_sub, (sb + 1) * keys_per_sub):
            for lt in range(tt // V7X_LANES):
                ls = slice(lt * V7X_LANES, (lt + 1) * V7X_LANES)
                bcast = lambda ref, head: jnp.broadcast_to(ref[head, il:il + 1, ls], (packed, V7X_LANES)).astype(BF16)
                lrow = [bcast(lrow_ref, head) for head in range(PEER_HEADS)]
                e1 = [bcast(e1_ref, head) for head in range(PEER_HEADS)]
                for rc in range(PEER_KEYS // packed):
                    js = slice(rc * packed, (rc + 1) * packed)
                    ns = slice(il * PEER_KEYS + rc * packed, il * PEER_KEYS + (rc + 1) * packed)
                    gate = None
                    for head in range(PEER_HEADS):
                        term = jnp.where(rank2_scr[head, js, ls] < lrow[head], e2_scr[head, js, ls], zero) * e1[head]
                        gate = term if gate is None else gate + term
                    hid = h_scr[ns, ls]
                    gelu = 0.5 * hid * (1.0 + lax.erf(hid * sqrt_half))
                    w_scr[ns, ls] = gate * gelu.astype(BF16)
        acc_ref[...] += jnp.dot(vt_ref[:, rows], w_scr[rows, :], preferred_element_type=F32)

    @pl.when(step == pl.num_programs(1) - 1)
    def _():
        y = alpha * xt_ref[...] + acc_ref[...]
        mu = jnp.mean(y, axis=0, keepdims=True)
        cen = y - mu
        var = jnp.mean(cen * cen, axis=0, keepdims=True)
        ln = cen * lax.rsqrt(var + LN_EPS) * g_ref[...] + b_ref[...]
        o_ref[...] = ln.T.astype(o_ref.dtype)


def _peer_experts(xt_bf16, xt_f32, u, v_t, rank2, e2, lrow, e1, ln_g, ln_b, alpha, out_dtype):
    d_model, tokens = xt_f32.shape
    experts = u.shape[0]
    tt = EXPERT_TOKENS
    eb = EXPERT_BLOCK
    keys_per_block = eb // PEER_KEYS
    full = pl.BlockSpec((PEER_HEADS, PEER_KEYS, tt), lambda t, e: (0, 0, t))
    rows = pl.BlockSpec((PEER_HEADS, keys_per_block, tt), lambda t, e: (0, e, t))
    col = pl.BlockSpec((d_model, 1), lambda t, e: (0, 0))
    return pl.pallas_call(
        functools.partial(_expert_kernel, alpha=alpha),
        grid=(tokens // tt, experts // eb),
        in_specs=[pl.BlockSpec((d_model, tt), lambda t, e: (0, t)),
                  pl.BlockSpec((d_model, tt), lambda t, e: (0, t)),
                  pl.BlockSpec((eb, d_model), lambda t, e: (e, 0)),
                  pl.BlockSpec((d_model, eb), lambda t, e: (0, e)),
                  full, full, rows, rows, col, col],
        out_specs=pl.BlockSpec((tt, d_model), lambda t, e: (t, 0)),
        out_shape=jax.ShapeDtypeStruct((tokens, d_model), out_dtype),
        scratch_shapes=[pltpu.VMEM((d_model, tt), F32), pltpu.VMEM((eb, tt), F32), pltpu.VMEM((eb, tt), BF16),
                        pltpu.VMEM((PEER_HEADS, PEER_KEYS, tt), BF16), pltpu.VMEM((PEER_HEADS, PEER_KEYS, tt), BF16)],
        compiler_params=pltpu.CompilerParams(
            dimension_semantics=("arbitrary", "arbitrary"),
            vmem_limit_bytes=_vmem_limit(
                _nbytes((d_model, tt), BF16) + _nbytes((d_model, tt), F32) + 2 * _nbytes((eb, d_model), BF16)
                + 2 * _nbytes((PEER_HEADS, PEER_KEYS, tt), BF16) + 2 * _nbytes((PEER_HEADS, keys_per_block, tt), F32)
                + 2 * _nbytes((d_model, V7X_LANES), F32) + _nbytes((tt, d_model), F32),
                _nbytes((d_model, tt), F32) + _nbytes((eb, tt), F32) + _nbytes((eb, tt), BF16)
                + 2 * _nbytes((PEER_HEADS, PEER_KEYS, tt), BF16) + 4 * _nbytes((d_model, tt), F32)),
        ),
    )(xt_bf16, xt_f32, u, v_t, rank2, e2, lrow, e1, ln_g, ln_b)


def _lambda_init(layer_idx):
    return 0.8 - 0.6 * math.exp(-0.3 * (layer_idx - 1))


def kernel(x, w_in, ret_gn_g, ret_gn_b, diff_lambda_q1, diff_lambda_k1, diff_lambda_q2, diff_lambda_k2,
           diff_subln_g, w_out, ln1_g, ln1_b, peer_w_query, peer_sub_keys, peer_u, peer_v, ln2_g, ln2_b):
    batch, seq, d_model = x.shape
    depth = w_in.shape[0]
    tokens = batch * seq
    alpha = (2.0 * depth) ** 0.25
    assert w_in.shape[2] == 7 * GROUP and d_model == 2 * GROUP
    assert seq % max(INPROJ_ROWS, RET_ROWS, DIFF_QBLOCK) == 0
    assert tokens % max(OUTPROJ_ROWS, ROUTE_TOKENS, EXPERT_TOKENS) == 0
    assert peer_u.shape[1] == PEER_KEYS * PEER_KEYS and peer_u.shape[1] % EXPERT_BLOCK == 0
    assert peer_sub_keys.shape[1:] == (PEER_HEADS, 2, PEER_KEYS, PEER_KEYS)

    cos_tab, sin_tab = _rope_tables(seq)
    x2d = x.reshape(tokens, d_model)
    for l in range(depth):
        lam_init = _lambda_init(l + 1)
        row = lambda p: p[l].reshape(1, -1)
        rq, rk, rv, rg, dq, dk, dv = _in_projection(x2d, w_in[l].astype(BF16), cos_tab, sin_tab, seq)
        r = _retention(rq, rk, rv, rg, row(ret_gn_g), row(ret_gn_b), batch, seq)
        d = _diff_attention(dq, dk, dv, row(diff_lambda_q1), row(diff_lambda_k1), row(diff_lambda_q2),
                            row(diff_lambda_k2), row(diff_subln_g), batch, seq, lam_init)
        xt_f32, xt_bf16 = _out_projection(r, d, x2d, w_out[l].astype(BF16), row(ln1_g), row(ln1_b), alpha)
        keys = peer_sub_keys[l].reshape(2 * PEER_HEADS, PEER_KEYS, PEER_KEYS).astype(BF16)
        rank2, e2, lrow, e1 = _peer_route(xt_bf16, peer_w_query[l].T.astype(BF16), keys)
        x2d = _peer_experts(xt_bf16, xt_f32, peer_u[l].astype(BF16), peer_v[l].T.astype(BF16),
                            rank2, e2, lrow, e1, ln2_g[l].reshape(-1, 1), ln2_b[l].reshape(-1, 1), alpha, x.dtype)
    return x2d.reshape(batch, seq, d_model)
```

```python
import functools
import math

import jax
import jax.numpy as jnp
import numpy as np
from jax import lax
from jax.experimental import pallas as pl
from jax.experimental.pallas import tpu as pltpu

F32 = jnp.float32
BF16 = jnp.bfloat16

V7X_LANES = 128
V7X_SUBLANES = 8
V7X_VMEM_BYTES = 64 * 1024 * 1024

RET_HEADS = 4
DIFF_HEADS = 4
HEAD_DIM = 128
RET_CHUNK = 128
ROPE_BASE = 10000.0
PEER_HEADS = 8
PEER_KEYS = 128
PEER_TOPK = 16
LN_EPS = 1e-5
GROUP = RET_HEADS * HEAD_DIM

INPROJ_ROWS = 512
RET_ROWS = 512
DIFF_QBLOCK = 256
OUTPROJ_ROWS = 512
ROUTE_TOKENS = 256
ROUTE_SLOTS_PER_LOOP = 2
EXPERT_TOKENS = 512
EXPERT_BLOCK = 1024
EXPERT_SUBBLOCK = 256
EXPERT_OUT_SUBBLOCKS = 2


def _vmem_limit(pipelined_bytes, resident_bytes):
    need = 2 * pipelined_bytes + resident_bytes
    return int(min(max(need, 16 * 1024 * 1024), V7X_VMEM_BYTES - 8 * 1024 * 1024))


def _nbytes(shape, dtype):
    return int(np.prod(shape)) * jnp.dtype(dtype).itemsize


def _rope_kernel(cos_ref, sin_ref):
    shape = cos_ref.shape
    half = HEAD_DIM // 2
    pos = lax.broadcasted_iota(jnp.int32, shape, 0).astype(F32)
    lane = lax.broadcasted_iota(jnp.int32, shape, 1)
    freq_idx = jnp.where(lane >= half, lane - half, lane).astype(F32)
    inv = jnp.exp(freq_idx * (-math.log(ROPE_BASE) / half))
    ang = pos * inv
    cos_ref[...] = jnp.cos(ang)
    sin = jnp.sin(ang)
    sin_ref[...] = jnp.where(lane >= half, sin, -sin)


def _rope_tables(seq):
    return pl.pallas_call(
        _rope_kernel,
        out_shape=(jax.ShapeDtypeStruct((seq, HEAD_DIM), F32),) * 2,
    )()


def _inproj_kernel(x_ref, w_ref, cos_ref, sin_ref, rq_ref, rk_ref, rv_ref, rg_ref, dq_ref, dk_ref, dv_ref):
    xb = x_ref[...].astype(BF16)
    cos = cos_ref[...]
    sin = sin_ref[...]

    def proj(group):
        return jnp.dot(xb, w_ref[:, group * GROUP:(group + 1) * GROUP], preferred_element_type=F32)

    def store_rotary(out_ref, p, scale):
        for h in range(RET_HEADS):
            ph = p[:, h * HEAD_DIM:(h + 1) * HEAD_DIM]
            rot = ph * cos + pltpu.roll(ph, HEAD_DIM // 2, axis=1) * sin
            if scale != 1.0:
                rot = rot * scale
            out_ref[:, h * HEAD_DIM:(h + 1) * HEAD_DIM] = rot.astype(out_ref.dtype)

    store_rotary(rq_ref, proj(0), 1.0)
    store_rotary(rk_ref, proj(1), HEAD_DIM ** -0.5)
    rv_ref[...] = proj(2).astype(rv_ref.dtype)
    rg_ref[...] = proj(3).astype(rg_ref.dtype)
    dq_ref[...] = (proj(4) * ((HEAD_DIM // 2) ** -0.5)).astype(dq_ref.dtype)
    dk_ref[...] = proj(5).astype(dk_ref.dtype)
    dv_ref[...] = proj(6).astype(dv_ref.dtype)


def _in_projection(x2d, w_in, cos_tab, sin_tab, seq):
    tokens, d_model = x2d.shape
    cols = w_in.shape[1]
    tm = INPROJ_ROWS
    pos_blocks = seq // tm
    out = jax.ShapeDtypeStruct((tokens, GROUP), BF16)
    group_spec = pl.BlockSpec((tm, GROUP), lambda i: (i, 0))
    tab_spec = pl.BlockSpec((tm, HEAD_DIM), lambda i: (i % pos_blocks, 0))
    return pl.pallas_call(
        _inproj_kernel,
        grid=(tokens // tm,),
        in_specs=[
            pl.BlockSpec((tm, d_model), lambda i: (i, 0)),
            pl.BlockSpec((d_model, cols), lambda i: (0, 0)),
            tab_spec,
            tab_spec,
        ],
        out_specs=[group_spec] * 7,
        out_shape=[out] * 7,
        compiler_params=pltpu.CompilerParams(
            dimension_semantics=("arbitrary",),
            vmem_limit_bytes=_vmem_limit(
                _nbytes((tm, d_model), F32) + _nbytes((d_model, cols), BF16) + 7 * _nbytes((tm, GROUP), BF16)
                + 2 * _nbytes((tm, HEAD_DIM), F32),
                _nbytes((tm, d_model), BF16) + 3 * _nbytes((tm, GROUP), F32)),
        ),
    )(x2d, w_in, cos_tab, sin_tab)


def _retention_kernel(q_ref, k_ref, v_ref, g_ref, gn_g_ref, gn_b_ref, o_ref, state_ref):
    @pl.when(pl.program_id(1) == 0)
    def _():
        state_ref[...] = jnp.zeros_like(state_ref)

    c = RET_CHUNK
    row = lax.broadcasted_iota(jnp.int32, (c, c), 0).astype(F32)
    col = lax.broadcasted_iota(jnp.int32, (c, c), 1).astype(F32)
    rel = row - col
    idx = lax.broadcasted_iota(jnp.int32, (c, 1), 0).astype(F32)
    for h in range(RET_HEADS):
        log_g = math.log1p(-(2.0 ** (-5.0 - h)))
        decay_in = jnp.where(rel >= 0, jnp.exp(jnp.maximum(rel, 0.0) * log_g), 0.0)
        decay_q = jnp.exp((idx + 1.0) * log_g)
        decay_k = jnp.exp((c - 1.0 - idx) * log_g)
        decay_c = math.exp(c * log_g)
        hs = slice(h * HEAD_DIM, (h + 1) * HEAD_DIM)
        gn_g = gn_g_ref[:, hs]
        gn_b = gn_b_ref[:, hs]
        state = state_ref[h]
        for ci in range(q_ref.shape[0] // c):
            rs = slice(ci * c, (ci + 1) * c)
            q = q_ref[rs, hs]
            k = k_ref[rs, hs]
            v = v_ref[rs, hs]
            s = lax.dot_general(q, k, (((1,), (1,)), ((), ())), preferred_element_type=F32) * decay_in
            inner = jnp.dot(s.astype(BF16), v, preferred_element_type=F32)
            cross = jnp.dot(q, state.astype(BF16), preferred_element_type=F32) * decay_q
            kd_t = (k.astype(F32) * decay_k).T.astype(BF16)
            state = state * decay_c + jnp.dot(kd_t, v, preferred_element_type=F32)
            o = inner + cross
            mu = jnp.mean(o, axis=-1, keepdims=True)
            cen = o - mu
            var = jnp.mean(cen * cen, axis=-1, keepdims=True)
            normed = cen * lax.rsqrt(var + LN_EPS) * gn_g + gn_b
            gate = g_ref[rs, hs].astype(F32)
            o_ref[rs, hs] = (gate * jax.nn.sigmoid(gate) * normed).astype(o_ref.dtype)
        state_ref[h] = state


def _retention(rq, rk, rv, rg, gn_g, gn_b, batch, seq):
    rows = RET_ROWS
    blocks = seq // rows
    spec = pl.BlockSpec((rows, GROUP), lambda b, c: (b * blocks + c, 0))
    par = pl.BlockSpec((1, GROUP), lambda b, c: (0, 0))
    return pl.pallas_call(
        _retention_kernel,
        grid=(batch, blocks),
        in_specs=[spec, spec, spec, spec, par, par],
        out_specs=spec,
        out_shape=jax.ShapeDtypeStruct(rq.shape, BF16),
        scratch_shapes=[pltpu.VMEM((RET_HEADS, HEAD_DIM, HEAD_DIM), F32)],
        compiler_params=pltpu.CompilerParams(
            dimension_semantics=("arbitrary", "arbitrary"),
            vmem_limit_bytes=_vmem_limit(5 * _nbytes((rows, GROUP), BF16), 4 * 1024 * 1024),
        ),
    )(rq, rk, rv, rg, gn_g, gn_b)


def _diff_kernel(q_ref, k_ref, v_ref, lq1_ref, lk1_ref, lq2_ref, lk2_ref, g_ref, o_ref, *, lam_init):
    f = lambda r: r[...].astype(F32)
    lam = (jnp.exp(jnp.sum(f(lq1_ref) * f(lk1_ref), axis=-1, keepdims=True))
           - jnp.exp(jnp.sum(f(lq2_ref) * f(lk2_ref), axis=-1, keepdims=True)) + lam_init)
    seq = q_ref.shape[0]
    bq = DIFF_QBLOCK
    half = HEAD_DIM // 2
    lane = lax.broadcasted_iota(jnp.int32, (bq, HEAD_DIM), 1)
    tri = (lax.broadcasted_iota(jnp.int32, (bq, bq), 1) <= lax.broadcasted_iota(jnp.int32, (bq, bq), 0))
    nt = (((1,), (1,)), ((), ()))
    zero = jnp.zeros((), BF16)
    for qi in range(seq // bq):
        past = qi * bq
        q = q_ref[past:past + bq, :]
        q_comp = (jnp.where(lane < half, q, zero), jnp.where(lane >= half, q, zero))
        k_diag = k_ref[past:past + bq, :]
        attn_diag = None
        attn_past = None
        for comp in range(2):
            s_diag = lax.dot_general(q_comp[comp], k_diag, nt, preferred_element_type=F32)
            s_diag = jnp.where(tri, s_diag, -jnp.inf)
            m = jnp.max(s_diag, axis=-1, keepdims=True)
            if past:
                s_past = lax.dot_general(q_comp[comp], k_ref[0:past, :], nt, preferred_element_type=F32)
                m = jnp.maximum(m, jnp.max(s_past, axis=-1, keepdims=True))
                e_past = jnp.exp(s_past - m)
            e_diag = jnp.exp(s_diag - m)
            denom = jnp.sum(e_diag, axis=-1, keepdims=True)
            if past:
                denom = denom + jnp.sum(e_past, axis=-1, keepdims=True)
            weight = (1.0 / denom) if comp == 0 else (-lam / denom)
            attn_diag = e_diag * weight if comp == 0 else attn_diag + e_diag * weight
            if past:
                attn_past = e_past * weight if comp == 0 else attn_past + e_past * weight
        o = jnp.dot(attn_diag.astype(BF16), v_ref[past:past + bq, :], preferred_element_type=F32)
        if past:
            o = o + jnp.dot(attn_past.astype(BF16), v_ref[0:past, :], preferred_element_type=F32)
        normed = o * lax.rsqrt(jnp.mean(o * o, axis=-1, keepdims=True) + LN_EPS)
        o_ref[past:past + bq, :] = (normed * g_ref[...] * (1.0 - lam_init)).astype(o_ref.dtype)


def _diff_attention(dq, dk, dv, lq1, lk1, lq2, lk2, subln_g, batch, seq, lam_init):
    spec = pl.BlockSpec((seq, HEAD_DIM), lambda b, h: (b, h))
    vec = pl.BlockSpec((1, HEAD_DIM // 2), lambda b, h: (0, 0))
    return pl.pallas_call(
        functools.partial(_diff_kernel, lam_init=lam_init),
        grid=(batch, DIFF_HEADS),
        in_specs=[spec, spec, spec, vec, vec, vec, vec, pl.BlockSpec((1, HEAD_DIM), lambda b, h: (0, h))],
        out_specs=spec,
        out_shape=jax.ShapeDtypeStruct(dq.shape, BF16),
        compiler_params=pltpu.CompilerParams(
            dimension_semantics=("arbitrary", "arbitrary"),
            vmem_limit_bytes=_vmem_limit(4 * _nbytes((seq, HEAD_DIM), BF16),
                                         8 * _nbytes((DIFF_QBLOCK, seq), F32)),
        ),
    )(dq, dk, dv, lq1, lk1, lq2, lk2, subln_g)


def _outproj_kernel(r_ref, d_ref, x_ref, w_ref, g_ref, b_ref, xt_ref, xtb_ref, *, alpha):
    mix = (jnp.dot(r_ref[...], w_ref[0:GROUP, :], preferred_element_type=F32)
           + jnp.dot(d_ref[...], w_ref[GROUP:2 * GROUP, :], preferred_element_type=F32))
    y = alpha * x_ref[...] + mix
    mu = jnp.mean(y, axis=-1, keepdims=True)
    cen = y - mu
    var = jnp.mean(cen * cen, axis=-1, keepdims=True)
    ln = cen * lax.rsqrt(var + LN_EPS) * g_ref[...] + b_ref[...]
    ln_t = ln.T
    xt_ref[...] = ln_t
    xtb_ref[...] = ln_t.astype(xtb_ref.dtype)


def _out_projection(r, d, x2d, w_out, ln_g, ln_b, alpha):
    tokens, d_model = x2d.shape
    tm = OUTPROJ_ROWS
    half = pl.BlockSpec((tm, GROUP), lambda i: (i, 0))
    par = pl.BlockSpec((1, d_model), lambda i: (0, 0))
    t_spec = pl.BlockSpec((d_model, tm), lambda i: (0, i))
    return pl.pallas_call(
        functools.partial(_outproj_kernel, alpha=alpha),
        grid=(tokens // tm,),
        in_specs=[half, half, pl.BlockSpec((tm, d_model), lambda i: (i, 0)),
                  pl.BlockSpec((2 * GROUP, d_model), lambda i: (0, 0)), par, par],
        out_specs=[t_spec, t_spec],
        out_shape=[jax.ShapeDtypeStruct((d_model, tokens), F32), jax.ShapeDtypeStruct((d_model, tokens), BF16)],
        compiler_params=pltpu.CompilerParams(
            dimension_semantics=("arbitrary",),
            vmem_limit_bytes=_vmem_limit(
                2 * _nbytes((tm, GROUP), BF16) + _nbytes((tm, d_model), F32) + _nbytes((2 * GROUP, d_model), BF16)
                + _nbytes((d_model, tm), F32) + _nbytes((d_model, tm), BF16),
                4 * _nbytes((tm, d_model), F32)),
        ),
    )(r, d, x2d, w_out, ln_g, ln_b)


def _candidate_cells():
    return [(p, q) for p in range(PEER_TOPK) for q in range(PEER_TOPK) if (p + 1) * (q + 1) <= PEER_TOPK]


def _route_kernel(xt_ref, wq_ref, keys_ref, rank2_ref, e2_ref, lrow_ref, e1_ref,
                  q_scr, score_scr, top_scr, idx_scr, len_scr, invz_scr):
    tt = xt_ref.shape[1]
    k = PEER_TOPK
    sub = V7X_SUBLANES
    groups = PEER_KEYS // sub
    lane_tiles = tt // V7X_LANES
    q_scr[...] = jnp.dot(wq_ref[...], xt_ref[...], preferred_element_type=F32).astype(BF16)
    group_iota = [(lax.broadcasted_iota(jnp.int32, (sub, V7X_LANES), 0) + g * sub).astype(F32) for g in range(groups)]

    for slot in range(2 * PEER_HEADS):
        src = (slot % PEER_HEADS) * 2 + slot // PEER_HEADS
        score_scr[slot] = jnp.dot(keys_ref[src], q_scr[src * PEER_KEYS:(src + 1) * PEER_KEYS, :],
                                  preferred_element_type=F32)

    for first_slot in range(0, 2 * PEER_HEADS, ROUTE_SLOTS_PER_LOOP):
        chains = [(slot, lt) for slot in range(first_slot, first_slot + ROUTE_SLOTS_PER_LOOP)
                  for lt in range(lane_tiles)]

        def extract(r, state, chains=chains):
            out = []
            for c, (slot, lt) in enumerate(chains):
                ls = slice(lt * V7X_LANES, (lt + 1) * V7X_LANES)
                cur = state[c * groups:(c + 1) * groups]
                vals, idxs = list(cur), list(group_iota)
                while len(vals) > 1:
                    left = [vals[a] >= vals[a + 1] for a in range(0, len(vals), 2)]
                    idxs = [jnp.where(left[a // 2], idxs[a], idxs[a + 1]) for a in range(0, len(vals), 2)]
                    vals = [jnp.maximum(vals[a], vals[a + 1]) for a in range(0, len(vals), 2)]
                v, i = vals[0], idxs[0]
                for shift in (4, 2, 1):
                    v_other = pltpu.roll(v, shift, axis=0)
                    i_other = pltpu.roll(i, shift, axis=0)
                    keep = (v > v_other) | ((v == v_other) & (i < i_other))
                    v = jnp.where(keep, v, v_other)
                    i = jnp.where(keep, i, i_other)
                top_scr[r, slot:slot + 1, ls] = v[0:1, :]
                idx_scr[r, slot:slot + 1, ls] = i[0:1, :]
                out.extend(jnp.where(group_iota[g] == i, -jnp.inf, cur[g]) for g in range(groups))
            return tuple(out)

        init = tuple(score_scr[slot, g * sub:(g + 1) * sub, lt * V7X_LANES:(lt + 1) * V7X_LANES]
                     for slot, lt in chains for g in range(groups))
        lax.fori_loop(0, k, extract, init)

    cells = _candidate_cells()
    a = [top_scr[p, 0:PEER_HEADS, :] for p in range(k)]
    b = [top_scr[q, PEER_HEADS:2 * PEER_HEADS, :] for q in range(k)]
    total = {cell: a[cell[0]] + b[cell[1]] for cell in cells}
    ahead = {cell: jnp.full((PEER_HEADS, tt), float((cell[0] + 1) * (cell[1] + 1) - 1), F32) for cell in cells}
    for xi, x in enumerate(cells):
        for y in cells[xi + 1:]:
            if (x[0] <= y[0] and x[1] <= y[1]) or (x[0] >= y[0] and x[1] >= y[1]):
                continue
            x_first = total[x] >= total[y]
            ahead[y] = ahead[y] + jnp.where(x_first, 1.0, 0.0)
            ahead[x] = ahead[x] + jnp.where(x_first, 0.0, 1.0)
    best = total[(0, 0)]
    denom = jnp.zeros((PEER_HEADS, tt), F32)
    counts = [jnp.zeros((PEER_HEADS, tt), F32) for _ in range(k)]
    for cell in cells:
        chosen = ahead[cell] < float(k)
        denom = denom + jnp.where(chosen, jnp.exp(total[cell] - best), 0.0)
        counts[cell[0]] = counts[cell[0]] + jnp.where(chosen, 1.0, 0.0)
    invz_scr[...] = 1.0 / denom
    for p in range(k):
        len_scr[p] = counts[p]

    key_iota = lax.broadcasted_iota(jnp.int32, (PEER_KEYS, tt), 0).astype(F32)
    for head in range(PEER_HEADS):
        hs = slice(head, head + 1)
        other = PEER_HEADS + head
        os_ = slice(other, other + 1)
        lrow = jnp.zeros((PEER_KEYS, tt), F32)
        rank2 = jnp.full((PEER_KEYS, tt), float(k), F32)
        for r in range(k):
            lrow = jnp.where(key_iota == idx_scr[r, hs, :], len_scr[r, hs, :], lrow)
            rank2 = jnp.where(key_iota == idx_scr[r, os_, :], float(r), rank2)
        lrow_ref[head] = lrow
        rank2_ref[head] = rank2.astype(rank2_ref.dtype)
        e1_ref[head] = jnp.exp(score_scr[head] - top_scr[0, hs, :]) * invz_scr[hs, :]
        e2_ref[head] = jnp.exp(score_scr[other] - top_scr[0, os_, :]).astype(e2_ref.dtype)


def _peer_route(xt_bf16, wq_t, keys):
    d_model, tokens = xt_bf16.shape
    tt = ROUTE_TOKENS
    qdim = wq_t.shape[0]
    slots = 2 * PEER_HEADS
    shape = (PEER_HEADS, PEER_KEYS, tokens)
    out_spec = pl.BlockSpec((PEER_HEADS, PEER_KEYS, tt), lambda t: (0, 0, t))
    return pl.pallas_call(
        _route_kernel,
        grid=(tokens // tt,),
        in_specs=[pl.BlockSpec((d_model, tt), lambda t: (0, t)),
                  pl.BlockSpec((qdim, d_model), lambda t: (0, 0)),
                  pl.BlockSpec((slots, PEER_KEYS, PEER_KEYS), lambda t: (0, 0, 0))],
        out_specs=[out_spec] * 4,
        out_shape=[jax.ShapeDtypeStruct(shape, BF16), jax.ShapeDtypeStruct(shape, BF16),
                   jax.ShapeDtypeStruct(shape, F32), jax.ShapeDtypeStruct(shape, F32)],
        scratch_shapes=[
            pltpu.VMEM((qdim, tt), BF16),
            pltpu.VMEM((slots, PEER_KEYS, tt), F32),
            pltpu.VMEM((PEER_TOPK, slots, tt), F32),
            pltpu.VMEM((PEER_TOPK, slots, tt), F32),
            pltpu.VMEM((PEER_TOPK, PEER_HEADS, tt), F32),
            pltpu.VMEM((PEER_HEADS, tt), F32),
        ],
        compiler_params=pltpu.CompilerParams(
            dimension_semantics=("arbitrary",),
            vmem_limit_bytes=_vmem_limit(
                _nbytes((d_model, tt), BF16) + _nbytes((qdim, d_model), BF16)
                + _nbytes((slots, PEER_KEYS, PEER_KEYS), BF16) + 3 * _nbytes((PEER_HEADS, PEER_KEYS, tt), F32),
                _nbytes((qdim, tt), F32) + 4 * _nbytes((slots, PEER_KEYS, tt), F32)),
        ),
    )(xt_bf16, wq_t, keys)


def _expert_kernel(xtb_ref, xt_ref, u_ref, vt_ref, rank2_ref, e2_ref, lrow_ref, e1_ref, g_ref, b_ref,
                   o_ref, acc_ref, h_scr, w_scr, gate_scr, *, alpha):
    step = pl.program_id(1)
    lane_tiles, row_tiles = h_scr.shape[0], h_scr.shape[1]
    eb = row_tiles * V7X_SUBLANES
    packed = 2 * V7X_SUBLANES
    key_chunks = PEER_KEYS // packed

    @pl.when(step == 0)
    def _():
        acc_ref[...] = jnp.zeros_like(acc_ref)
        for rc in range(key_chunks):
            for lt in range(lane_tiles):
                for head in range(PEER_HEADS):
                    src = (head, slice(rc * packed, (rc + 1) * packed), slice(lt * V7X_LANES, (lt + 1) * V7X_LANES))
                    gate_scr[rc, lt, 2 * head] = rank2_ref[src]
                    gate_scr[rc, lt, 2 * head + 1] = e2_ref[src]

    keys_per_sub = EXPERT_SUBBLOCK // PEER_KEYS
    tiles_per_sub = EXPERT_SUBBLOCK // V7X_SUBLANES
    sqrt_half = float(np.float32(np.sqrt(0.5)))
    zero = jnp.zeros((), BF16)
    sub_blocks = eb // EXPERT_SUBBLOCK

    def hidden(sb):
        rows = slice(sb * EXPERT_SUBBLOCK, (sb + 1) * EXPERT_SUBBLOCK)
        hid = jnp.dot(u_ref[rows, :], xtb_ref[...], preferred_element_type=F32)
        for rt in range(tiles_per_sub):
            for lt in range(lane_tiles):
                h_scr[lt, sb * tiles_per_sub + rt] = hid[rt * V7X_SUBLANES:(rt + 1) * V7X_SUBLANES,
                                                        lt * V7X_LANES:(lt + 1) * V7X_LANES]

    def gelu(hid):
        return 0.5 * hid * (1.0 + lax.erf(hid * sqrt_half))

    hidden(0)
    for sb in range(sub_blocks):
        rows = slice(sb * EXPERT_SUBBLOCK, (sb + 1) * EXPERT_SUBBLOCK)
        if sb + 1 < sub_blocks:
            hidden(sb + 1)
        for il in range(sb * keys_per_sub, (sb + 1) * keys_per_sub):
            for lt in range(lane_tiles):
                ls = slice(lt * V7X_LANES, (lt + 1) * V7X_LANES)
                bcast = lambda ref, head: jnp.broadcast_to(ref[head, il:il + 1, ls], (packed, V7X_LANES)).astype(BF16)
                gate = [None] * key_chunks
                for head in range(PEER_HEADS):
                    lrow = bcast(lrow_ref, head)
                    e1 = bcast(e1_ref, head)
                    for rc in range(key_chunks):
                        term = jnp.where(gate_scr[rc, lt, 2 * head] < lrow, gate_scr[rc, lt, 2 * head + 1], zero) * e1
                        gate[rc] = term if gate[rc] is None else gate[rc] + term
                for rc in range(key_chunks):
                    rt = (il * PEER_KEYS + rc * packed) // V7X_SUBLANES
                    act = jnp.concatenate([gelu(h_scr[lt, rt]), gelu(h_scr[lt, rt + 1])], axis=0)
                    w_scr[rt * V7X_SUBLANES:rt * V7X_SUBLANES + packed, ls] = gate[rc] * act.astype(BF16)
        if (sb + 1) % EXPERT_OUT_SUBBLOCKS == 0:
            rows = slice((sb + 1 - EXPERT_OUT_SUBBLOCKS) * EXPERT_SUBBLOCK, (sb + 1) * EXPERT_SUBBLOCK)
            acc_ref[...] += jnp.dot(vt_ref[:, rows], w_scr[rows, :], preferred_element_type=F32)

    @pl.when(step == pl.num_programs(1) - 1)
    def _():
        y = alpha * xt_ref[...] + acc_ref[...]
        mu = jnp.mean(y, axis=0, keepdims=True)
        cen = y - mu
        var = jnp.mean(cen * cen, axis=0, keepdims=True)
        ln = cen * lax.rsqrt(var + LN_EPS) * g_ref[...] + b_ref[...]
        o_ref[...] = ln.T.astype(o_ref.dtype)


def _peer_experts(xt_bf16, xt_f32, u, v_t, rank2, e2, lrow, e1, ln_g, ln_b, alpha, out_dtype):
    d_model, tokens = xt_f32.shape
    experts = u.shape[0]
    tt = EXPERT_TOKENS
    eb = EXPERT_BLOCK
    keys_per_block = eb // PEER_KEYS
    full = pl.BlockSpec((PEER_HEADS, PEER_KEYS, tt), lambda t, e: (0, 0, t))
    rows = pl.BlockSpec((PEER_HEADS, keys_per_block, tt), lambda t, e: (0, e, t))
    col = pl.BlockSpec((d_model, 1), lambda t, e: (0, 0))
    return pl.pallas_call(
        functools.partial(_expert_kernel, alpha=alpha),
        grid=(tokens // tt, experts // eb),
        in_specs=[pl.BlockSpec((d_model, tt), lambda t, e: (0, t)),
                  pl.BlockSpec((d_model, tt), lambda t, e: (0, t)),
                  pl.BlockSpec((eb, d_model), lambda t, e: (e, 0)),
                  pl.BlockSpec((d_model, eb), lambda t, e: (0, e)),
                  full, full, rows, rows, col, col],
        out_specs=pl.BlockSpec((tt, d_model), lambda t, e: (t, 0)),
        out_shape=jax.ShapeDtypeStruct((tokens, d_model), out_dtype),
        scratch_shapes=[pltpu.VMEM((d_model, tt), F32),
                        pltpu.VMEM((tt // V7X_LANES, eb // V7X_SUBLANES, V7X_SUBLANES, V7X_LANES), F32),
                        pltpu.VMEM((eb, tt), BF16),
                        pltpu.VMEM((PEER_KEYS // (2 * V7X_SUBLANES), tt // V7X_LANES, 2 * PEER_HEADS,
                                    2 * V7X_SUBLANES, V7X_LANES), BF16)],
        compiler_params=pltpu.CompilerParams(
            dimension_semantics=("arbitrary", "arbitrary"),
            vmem_limit_bytes=_vmem_limit(
                _nbytes((d_model, tt), BF16) + _nbytes((d_model, tt), F32) + 2 * _nbytes((eb, d_model), BF16)
                + 2 * _nbytes((PEER_HEADS, PEER_KEYS, tt), BF16) + 2 * _nbytes((PEER_HEADS, keys_per_block, tt), F32)
                + 2 * _nbytes((d_model, V7X_LANES), F32) + _nbytes((tt, d_model), F32),
                _nbytes((d_model, tt), F32) + _nbytes((eb, tt), F32) + _nbytes((eb, tt), BF16)
                + 2 * _nbytes((PEER_HEADS, PEER_KEYS, tt), BF16) + 4 * _nbytes((d_model, tt), F32)),
        ),
    )(xt_bf16, xt_f32, u, v_t, rank2, e2, lrow, e1, ln_g, ln_b)


def _lambda_init(layer_idx):
    return 0.8 - 0.6 * math.exp(-0.3 * (layer_idx - 1))


def kernel(x, w_in, ret_gn_g, ret_gn_b, diff_lambda_q1, diff_lambda_k1, diff_lambda_q2, diff_lambda_k2,
           diff_subln_g, w_out, ln1_g, ln1_b, peer_w_query, peer_sub_keys, peer_u, peer_v, ln2_g, ln2_b):
    batch, seq, d_model = x.shape
    depth = w_in.shape[0]
    tokens = batch * seq
    alpha = (2.0 * depth) ** 0.25
    assert w_in.shape[2] == 7 * GROUP and d_model == 2 * GROUP
    assert seq % max(INPROJ_ROWS, RET_ROWS, DIFF_QBLOCK) == 0
    assert tokens % max(OUTPROJ_ROWS, ROUTE_TOKENS, EXPERT_TOKENS) == 0
    assert peer_u.shape[1] == PEER_KEYS * PEER_KEYS and peer_u.shape[1] % EXPERT_BLOCK == 0
    assert peer_sub_keys.shape[1:] == (PEER_HEADS, 2, PEER_KEYS, PEER_KEYS)

    cos_tab, sin_tab = _rope_tables(seq)
    x2d = x.reshape(tokens, d_model)
    for l in range(depth):
        lam_init = _lambda_init(l + 1)
        row = lambda p: p[l].reshape(1, -1)
        rq, rk, rv, rg, dq, dk, dv = _in_projection(x2d, w_in[l].astype(BF16), cos_tab, sin_tab, seq)
        r = _retention(rq, rk, rv, rg, row(ret_gn_g), row(ret_gn_b), batch, seq)
        d = _diff_attention(dq, dk, dv, row(diff_lambda_q1), row(diff_lambda_k1), row(diff_lambda_q2),
                            row(diff_lambda_k2), row(diff_subln_g), batch, seq, lam_init)
        xt_f32, xt_bf16 = _out_projection(r, d, x2d, w_out[l].astype(BF16), row(ln1_g), row(ln1_b), alpha)
        keys = peer_sub_keys[l].reshape(2 * PEER_HEADS, PEER_KEYS, PEER_KEYS).astype(BF16)
        rank2, e2, lrow, e1 = _peer_route(xt_bf16, peer_w_query[l].T.astype(BF16), keys)
        x2d = _peer_experts(xt_bf16, xt_f32, peer_u[l].astype(BF16), peer_v[l].T.astype(BF16),
                            rank2, e2, lrow, e1, ln2_g[l].reshape(-1, 1), ln2_b[l].reshape(-1, 1), alpha, x.dtype)
    return x2d.reshape(batch, seq, d_model)
```

```python
import functools
import math

import jax
import jax.numpy as jnp
import numpy as np
from jax import lax
from jax.experimental import pallas as pl
from jax.experimental.pallas import tpu as pltpu

F32 = jnp.float32
BF16 = jnp.bfloat16

V7X_LANES = 128
V7X_SUBLANES = 8
V7X_VMEM_BYTES = 64 * 1024 * 1024

RET_HEADS = 4
DIFF_HEADS = 4
HEAD_DIM = 128
RET_CHUNK = 128
ROPE_BASE = 10000.0
PEER_HEADS = 8
PEER_KEYS = 128
PEER_TOPK = 16
LN_EPS = 1e-5
GROUP = RET_HEADS * HEAD_DIM

INPROJ_ROWS = 512
RET_ROWS = 512
DIFF_QBLOCK = 256
OUTPROJ_ROWS = 512
ROUTE_TOKENS = 256
EXPERT_TOKENS = 512
EXPERT_BLOCK = 2048
EXPERT_SUBBLOCK = 256
EXPERT_OUT_SUBBLOCKS = 1


def _vmem_limit(pipelined_bytes, resident_bytes):
    need = 2 * pipelined_bytes + resident_bytes
    return int(min(max(need, 16 * 1024 * 1024), V7X_VMEM_BYTES - 8 * 1024 * 1024))


def _nbytes(shape, dtype):
    return int(np.prod(shape)) * jnp.dtype(dtype).itemsize


def _rope_kernel(cos_ref, sin_ref):
    shape = cos_ref.shape
    half = HEAD_DIM // 2
    pos = lax.broadcasted_iota(jnp.int32, shape, 0).astype(F32)
    lane = lax.broadcasted_iota(jnp.int32, shape, 1)
    freq_idx = jnp.where(lane >= half, lane - half, lane).astype(F32)
    inv = jnp.exp(freq_idx * (-math.log(ROPE_BASE) / half))
    ang = pos * inv
    cos_ref[...] = jnp.cos(ang)
    sin = jnp.sin(ang)
    sin_ref[...] = jnp.where(lane >= half, sin, -sin)


def _rope_tables(seq):
    return pl.pallas_call(
        _rope_kernel,
        out_shape=(jax.ShapeDtypeStruct((seq, HEAD_DIM), F32),) * 2,
    )()


def _inproj_kernel(x_ref, w_ref, cos_ref, sin_ref, rq_ref, rk_ref, rv_ref, rg_ref, dq_ref, dk_ref, dv_ref):
    xb = x_ref[...].astype(BF16)
    cos = cos_ref[...]
    sin = sin_ref[...]

    def proj(group):
        return jnp.dot(xb, w_ref[:, group * GROUP:(group + 1) * GROUP], preferred_element_type=F32)

    def store_rotary(out_ref, p, scale):
        for h in range(RET_HEADS):
            ph = p[:, h * HEAD_DIM:(h + 1) * HEAD_DIM]
            rot = ph * cos + pltpu.roll(ph, HEAD_DIM // 2, axis=1) * sin
            if scale != 1.0:
                rot = rot * scale
            out_ref[:, h * HEAD_DIM:(h + 1) * HEAD_DIM] = rot.astype(out_ref.dtype)

    store_rotary(rq_ref, proj(0), 1.0)
    store_rotary(rk_ref, proj(1), HEAD_DIM ** -0.5)
    rv_ref[...] = proj(2).astype(rv_ref.dtype)
    rg_ref[...] = proj(3).astype(rg_ref.dtype)
    dq_ref[...] = (proj(4) * ((HEAD_DIM // 2) ** -0.5)).astype(dq_ref.dtype)
    dk_ref[...] = proj(5).astype(dk_ref.dtype)
    dv_ref[...] = proj(6).astype(dv_ref.dtype)


def _in_projection(x2d, w_in, cos_tab, sin_tab, seq):
    tokens, d_model = x2d.shape
    cols = w_in.shape[1]
    tm = INPROJ_ROWS
    pos_blocks = seq // tm
    out = jax.ShapeDtypeStruct((tokens, GROUP), BF16)
    group_spec = pl.BlockSpec((tm, GROUP), lambda i: (i, 0))
    tab_spec = pl.BlockSpec((tm, HEAD_DIM), lambda i: (i % pos_blocks, 0))
    return pl.pallas_call(
        _inproj_kernel,
        grid=(tokens // tm,),
        in_specs=[
            pl.BlockSpec((tm, d_model), lambda i: (i, 0)),
            pl.BlockSpec((d_model, cols), lambda i: (0, 0)),
            tab_spec,
            tab_spec,
        ],
        out_specs=[group_spec] * 7,
        out_shape=[out] * 7,
        compiler_params=pltpu.CompilerParams(
            dimension_semantics=("arbitrary",),
            vmem_limit_bytes=_vmem_limit(
                _nbytes((tm, d_model), F32) + _nbytes((d_model, cols), BF16) + 7 * _nbytes((tm, GROUP), BF16)
                + 2 * _nbytes((tm, HEAD_DIM), F32),
                _nbytes((tm, d_model), BF16) + 3 * _nbytes((tm, GROUP), F32)),
        ),
    )(x2d, w_in, cos_tab, sin_tab)


def _retention_kernel(q_ref, k_ref, v_ref, g_ref, gn_g_ref, gn_b_ref, o_ref, state_ref):
    @pl.when(pl.program_id(1) == 0)
    def _():
        state_ref[...] = jnp.zeros_like(state_ref)

    c = RET_CHUNK
    row = lax.broadcasted_iota(jnp.int32, (c, c), 0).astype(F32)
    col = lax.broadcasted_iota(jnp.int32, (c, c), 1).astype(F32)
    rel = row - col
    idx = lax.broadcasted_iota(jnp.int32, (c, 1), 0).astype(F32)
    for h in range(RET_HEADS):
        log_g = math.log1p(-(2.0 ** (-5.0 - h)))
        decay_in = jnp.where(rel >= 0, jnp.exp(jnp.maximum(rel, 0.0) * log_g), 0.0)
        decay_q = jnp.exp((idx + 1.0) * log_g)
        decay_k = jnp.exp((c - 1.0 - idx) * log_g)
        decay_c = math.exp(c * log_g)
        hs = slice(h * HEAD_DIM, (h + 1) * HEAD_DIM)
        gn_g = gn_g_ref[:, hs]
        gn_b = gn_b_ref[:, hs]
        state = state_ref[h]
        for ci in range(q_ref.shape[0] // c):
            rs = slice(ci * c, (ci + 1) * c)
            q = q_ref[rs, hs]
            k = k_ref[rs, hs]
            v = v_ref[rs, hs]
            s = lax.dot_general(q, k, (((1,), (1,)), ((), ())), preferred_element_type=F32) * decay_in
            inner = jnp.dot(s.astype(BF16), v, preferred_element_type=F32)
            cross = jnp.dot(q, state.astype(BF16), preferred_element_type=F32) * decay_q
            kd_t = (k.astype(F32) * decay_k).T.astype(BF16)
            state = state * decay_c + jnp.dot(kd_t, v, preferred_element_type=F32)
            o = inner + cross
            mu = jnp.mean(o, axis=-1, keepdims=True)
            cen = o - mu
            var = jnp.mean(cen * cen, axis=-1, keepdims=True)
            normed = cen * lax.rsqrt(var + LN_EPS) * gn_g + gn_b
            gate = g_ref[rs, hs].astype(F32)
            o_ref[rs, hs] = (gate * jax.nn.sigmoid(gate) * normed).astype(o_ref.dtype)
        state_ref[h] = state


def _retention(rq, rk, rv, rg, gn_g, gn_b, batch, seq):
    rows = RET_ROWS
    blocks = seq // rows
    spec = pl.BlockSpec((rows, GROUP), lambda b, c: (b * blocks + c, 0))
    par = pl.BlockSpec((1, GROUP), lambda b, c: (0, 0))
    return pl.pallas_call(
        _retention_kernel,
        grid=(batch, blocks),
        in_specs=[spec, spec, spec, spec, par, par],
        out_specs=spec,
        out_shape=jax.ShapeDtypeStruct(rq.shape, BF16),
        scratch_shapes=[pltpu.VMEM((RET_HEADS, HEAD_DIM, HEAD_DIM), F32)],
        compiler_params=pltpu.CompilerParams(
            dimension_semantics=("arbitrary", "arbitrary"),
            vmem_limit_bytes=_vmem_limit(5 * _nbytes((rows, GROUP), BF16), 4 * 1024 * 1024),
        ),
    )(rq, rk, rv, rg, gn_g, gn_b)


def _diff_kernel(q_ref, k_ref, v_ref, lq1_ref, lk1_ref, lq2_ref, lk2_ref, g_ref, o_ref, *, lam_init):
    f = lambda r: r[...].astype(F32)
    lam = (jnp.exp(jnp.sum(f(lq1_ref) * f(lk1_ref), axis=-1, keepdims=True))
           - jnp.exp(jnp.sum(f(lq2_ref) * f(lk2_ref), axis=-1, keepdims=True)) + lam_init)
    seq = q_ref.shape[0]
    bq = DIFF_QBLOCK
    half = HEAD_DIM // 2
    lane = lax.broadcasted_iota(jnp.int32, (bq, HEAD_DIM), 1)
    tri = (lax.broadcasted_iota(jnp.int32, (bq, bq), 1) <= lax.broadcasted_iota(jnp.int32, (bq, bq), 0))
    nt = (((1,), (1,)), ((), ()))
    zero = jnp.zeros((), BF16)
    for qi in range(seq // bq):
        past = qi * bq
        q = q_ref[past:past + bq, :]
        q_comp = (jnp.where(lane < half, q, zero), jnp.where(lane >= half, q, zero))
        k_diag = k_ref[past:past + bq, :]
        attn_diag = None
        attn_past = None
        for comp in range(2):
            s_diag = lax.dot_general(q_comp[comp], k_diag, nt, preferred_element_type=F32)
            s_diag = jnp.where(tri, s_diag, -jnp.inf)
            m = jnp.max(s_diag, axis=-1, keepdims=True)
            if past:
                s_past = lax.dot_general(q_comp[comp], k_ref[0:past, :], nt, preferred_element_type=F32)
                m = jnp.maximum(m, jnp.max(s_past, axis=-1, keepdims=True))
                e_past = jnp.exp(s_past - m)
            e_diag = jnp.exp(s_diag - m)
            denom = jnp.sum(e_diag, axis=-1, keepdims=True)
            if past:
                denom = denom + jnp.sum(e_past, axis=-1, keepdims=True)
            weight = (1.0 / denom) if comp == 0 else (-lam / denom)
            attn_diag = e_diag * weight if comp == 0 else attn_diag + e_diag * weight
            if past:
                attn_past = e_past * weight if comp == 0 else attn_past + e_past * weight
        o = jnp.dot(attn_diag.astype(BF16), v_ref[past:past + bq, :], preferred_element_type=F32)
        if past:
            o = o + jnp.dot(attn_past.astype(BF16), v_ref[0:past, :], preferred_element_type=F32)
        normed = o * lax.rsqrt(jnp.mean(o * o, axis=-1, keepdims=True) + LN_EPS)
        o_ref[past:past + bq, :] = (normed * g_ref[...] * (1.0 - lam_init)).astype(o_ref.dtype)


def _diff_attention(dq, dk, dv, lq1, lk1, lq2, lk2, subln_g, batch, seq, lam_init):
    spec = pl.BlockSpec((seq, HEAD_DIM), lambda b, h: (b, h))
    vec = pl.BlockSpec((1, HEAD_DIM // 2), lambda b, h: (0, 0))
    return pl.pallas_call(
        functools.partial(_diff_kernel, lam_init=lam_init),
        grid=(batch, DIFF_HEADS),
        in_specs=[spec, spec, spec, vec, vec, vec, vec, pl.BlockSpec((1, HEAD_DIM), lambda b, h: (0, h))],
        out_specs=spec,
        out_shape=jax.ShapeDtypeStruct(dq.shape, BF16),
        compiler_params=pltpu.CompilerParams(
            dimension_semantics=("arbitrary", "arbitrary"),
            vmem_limit_bytes=_vmem_limit(4 * _nbytes((seq, HEAD_DIM), BF16),
                                         8 * _nbytes((DIFF_QBLOCK, seq), F32)),
        ),
    )(dq, dk, dv, lq1, lk1, lq2, lk2, subln_g)


def _outproj_kernel(r_ref, d_ref, x_ref, w_ref, g_ref, b_ref, xt_ref, xtb_ref, *, alpha):
    mix = (jnp.dot(r_ref[...], w_ref[0:GROUP, :], preferred_element_type=F32)
           + jnp.dot(d_ref[...], w_ref[GROUP:2 * GROUP, :], preferred_element_type=F32))
    y = alpha * x_ref[...] + mix
    mu = jnp.mean(y, axis=-1, keepdims=True)
    cen = y - mu
    var = jnp.mean(cen * cen, axis=-1, keepdims=True)
    ln = cen * lax.rsqrt(var + LN_EPS) * g_ref[...] + b_ref[...]
    ln_t = ln.T
    xt_ref[...] = ln_t
    xtb_ref[...] = ln_t.astype(xtb_ref.dtype)


def _out_projection(r, d, x2d, w_out, ln_g, ln_b, alpha):
    tokens, d_model = x2d.shape
    tm = OUTPROJ_ROWS
    half = pl.BlockSpec((tm, GROUP), lambda i: (i, 0))
    par = pl.BlockSpec((1, d_model), lambda i: (0, 0))
    t_spec = pl.BlockSpec((d_model, tm), lambda i: (0, i))
    return pl.pallas_call(
        functools.partial(_outproj_kernel, alpha=alpha),
        grid=(tokens // tm,),
        in_specs=[half, half, pl.BlockSpec((tm, d_model), lambda i: (i, 0)),
                  pl.BlockSpec((2 * GROUP, d_model), lambda i: (0, 0)), par, par],
        out_specs=[t_spec, t_spec],
        out_shape=[jax.ShapeDtypeStruct((d_model, tokens), F32), jax.ShapeDtypeStruct((d_model, tokens), BF16)],
        compiler_params=pltpu.CompilerParams(
            dimension_semantics=("arbitrary",),
            vmem_limit_bytes=_vmem_limit(
                2 * _nbytes((tm, GROUP), BF16) + _nbytes((tm, d_model), F32) + _nbytes((2 * GROUP, d_model), BF16)
                + _nbytes((d_model, tm), F32) + _nbytes((d_model, tm), BF16),
                4 * _nbytes((tm, d_model), F32)),
        ),
    )(r, d, x2d, w_out, ln_g, ln_b)


def _candidate_cells():
    return [(p, q) for p in range(PEER_TOPK) for q in range(PEER_TOPK) if (p + 1) * (q + 1) <= PEER_TOPK]


def _route_kernel(xt_ref, wq_ref, keys_ref, rank2_ref, e2_ref, lrow_ref, e1_ref,
                  q_scr, score_scr, work_scr, top_scr, idx_scr, len_scr, invz_scr):
    tt = xt_ref.shape[1]
    k = PEER_TOPK
    sub = V7X_SUBLANES
    groups = PEER_KEYS // sub
    lane_tiles = tt // V7X_LANES
    q_scr[...] = jnp.dot(wq_ref[...], xt_ref[...], preferred_element_type=F32).astype(BF16)
    group_iota = [(lax.broadcasted_iota(jnp.int32, (sub, V7X_LANES), 0) + g * sub).astype(F32) for g in range(groups)]

    chains = [(slot, lt) for slot in range(2 * PEER_HEADS) for lt in range(lane_tiles)]
    for slot in range(2 * PEER_HEADS):
        src = (slot % PEER_HEADS) * 2 + slot // PEER_HEADS
        scores = jnp.dot(keys_ref[src], q_scr[src * PEER_KEYS:(src + 1) * PEER_KEYS, :],
                         preferred_element_type=F32)
        score_scr[slot] = scores
        for lt in range(lane_tiles):
            for g in range(groups):
                work_scr[chains.index((slot, lt)), g] = scores[g * sub:(g + 1) * sub,
                                                               lt * V7X_LANES:(lt + 1) * V7X_LANES]

    def extract(r, carry):
        for c, (slot, lt) in enumerate(chains):
            ls = slice(lt * V7X_LANES, (lt + 1) * V7X_LANES)
            cur = [work_scr[c, g] for g in range(groups)]
            vals, idxs = list(cur), list(group_iota)
            while len(vals) > 1:
                left = [vals[a] >= vals[a + 1] for a in range(0, len(vals), 2)]
                idxs = [jnp.where(left[a // 2], idxs[a], idxs[a + 1]) for a in range(0, len(vals), 2)]
                vals = [jnp.maximum(vals[a], vals[a + 1]) for a in range(0, len(vals), 2)]
            v, i = vals[0], idxs[0]
            for shift in (4, 2, 1):
                v_other = pltpu.roll(v, shift, axis=0)
                i_other = pltpu.roll(i, shift, axis=0)
                keep = (v > v_other) | ((v == v_other) & (i < i_other))
                v = jnp.where(keep, v, v_other)
                i = jnp.where(keep, i, i_other)
            top_scr[r, slot:slot + 1, ls] = v[0:1, :]
            idx_scr[r, slot:slot + 1, ls] = i[0:1, :]
            for g in range(groups):
                work_scr[c, g] = jnp.where(group_iota[g] == i, -jnp.inf, cur[g])
        return carry

    lax.fori_loop(0, k, extract, 0)

    cells = _candidate_cells()
    a = [top_scr[p, 0:PEER_HEADS, :] for p in range(k)]
    b = [top_scr[q, PEER_HEADS:2 * PEER_HEADS, :] for q in range(k)]
    total = {cell: a[cell[0]] + b[cell[1]] for cell in cells}
    ahead = {cell: jnp.full((PEER_HEADS, tt), float((cell[0] + 1) * (cell[1] + 1) - 1), F32) for cell in cells}
    for xi, x in enumerate(cells):
        for y in cells[xi + 1:]:
            if (x[0] <= y[0] and x[1] <= y[1]) or (x[0] >= y[0] and x[1] >= y[1]):
                continue
            x_first = total[x] >= total[y]
            ahead[y] = ahead[y] + jnp.where(x_first, 1.0, 0.0)
            ahead[x] = ahead[x] + jnp.where(x_first, 0.0, 1.0)
    best = total[(0, 0)]
    denom = jnp.zeros((PEER_HEADS, tt), F32)
    counts = [jnp.zeros((PEER_HEADS, tt), F32) for _ in range(k)]
    for cell in cells:
        chosen = ahead[cell] < float(k)
        denom = denom + jnp.where(chosen, jnp.exp(total[cell] - best), 0.0)
        counts[cell[0]] = counts[cell[0]] + jnp.where(chosen, 1.0, 0.0)
    invz_scr[...] = 1.0 / denom
    for p in range(k):
        len_scr[p] = counts[p]

    key_iota = lax.broadcasted_iota(jnp.int32, (PEER_KEYS, tt), 0).astype(F32)
    for head in range(PEER_HEADS):
        hs = slice(head, head + 1)
        other = PEER_HEADS + head
        os_ = slice(other, other + 1)
        lrow = jnp.zeros((PEER_KEYS, tt), F32)
        rank2 = jnp.full((PEER_KEYS, tt), float(k), F32)
        for r in range(k):
            lrow = jnp.where(key_iota == idx_scr[r, hs, :], len_scr[r, hs, :], lrow)
            rank2 = jnp.where(key_iota == idx_scr[r, os_, :], float(r), rank2)
        lrow_ref[head] = lrow
        rank2_ref[head] = rank2.astype(rank2_ref.dtype)
        e1_ref[head] = jnp.exp(score_scr[head] - top_scr[0, hs, :]) * invz_scr[hs, :]
        e2_ref[head] = jnp.exp(score_scr[other] - top_scr[0, os_, :]).astype(e2_ref.dtype)


def _peer_route(xt_bf16, wq_t, keys):
    d_model, tokens = xt_bf16.shape
    tt = ROUTE_TOKENS
    qdim = wq_t.shape[0]
    slots = 2 * PEER_HEADS
    shape = (PEER_HEADS, PEER_KEYS, tokens)
    out_spec = pl.BlockSpec((PEER_HEADS, PEER_KEYS, tt), lambda t: (0, 0, t))
    return pl.pallas_call(
        _route_kernel,
        grid=(tokens // tt,),
        in_specs=[pl.BlockSpec((d_model, tt), lambda t: (0, t)),
                  pl.BlockSpec((qdim, d_model), lambda t: (0, 0)),
                  pl.BlockSpec((slots, PEER_KEYS, PEER_KEYS), lambda t: (0, 0, 0))],
        out_specs=[out_spec] * 4,
        out_shape=[jax.ShapeDtypeStruct(shape, BF16), jax.ShapeDtypeStruct(shape, BF16),
                   jax.ShapeDtypeStruct(shape, F32), jax.ShapeDtypeStruct(shape, F32)],
        scratch_shapes=[
            pltpu.VMEM((qdim, tt), BF16),
            pltpu.VMEM((slots, PEER_KEYS, tt), F32),
            pltpu.VMEM((slots * (tt // V7X_LANES), PEER_KEYS // V7X_SUBLANES, V7X_SUBLANES, V7X_LANES), F32),
            pltpu.VMEM((PEER_TOPK, slots, tt), F32),
            pltpu.VMEM((PEER_TOPK, slots, tt), F32),
            pltpu.VMEM((PEER_TOPK, PEER_HEADS, tt), F32),
            pltpu.VMEM((PEER_HEADS, tt), F32),
        ],
        compiler_params=pltpu.CompilerParams(
            dimension_semantics=("arbitrary",),
            vmem_limit_bytes=_vmem_limit(
                _nbytes((d_model, tt), BF16) + _nbytes((qdim, d_model), BF16)
                + _nbytes((slots, PEER_KEYS, PEER_KEYS), BF16) + 3 * _nbytes((PEER_HEADS, PEER_KEYS, tt), F32),
                _nbytes((qdim, tt), F32) + 4 * _nbytes((slots, PEER_KEYS, tt), F32)),
        ),
    )(xt_bf16, wq_t, keys)


def _expert_kernel(xtb_ref, xt_ref, u_ref, vt_ref, rank2_ref, e2_ref, lrow_ref, e1_ref, g_ref, b_ref,
                   o_ref, acc_ref, h_scr, w_scr, gate_scr, *, alpha):
    step = pl.program_id(1)
    lane_tiles, row_tiles = h_scr.shape[0], h_scr.shape[1]
    eb = row_tiles * V7X_SUBLANES
    packed = 2 * V7X_SUBLANES
    key_chunks = PEER_KEYS // packed

    @pl.when(step == 0)
    def _():
        acc_ref[...] = jnp.zeros_like(acc_ref)
        for rc in range(key_chunks):
            for lt in range(lane_tiles):
                for head in range(PEER_HEADS):
                    src = (head, slice(rc * packed, (rc + 1) * packed), slice(lt * V7X_LANES, (lt + 1) * V7X_LANES))
                    gate_scr[rc, lt, 2 * head] = rank2_ref[src]
                    gate_scr[rc, lt, 2 * head + 1] = e2_ref[src]

    keys_per_sub = EXPERT_SUBBLOCK // PEER_KEYS
    tiles_per_sub = EXPERT_SUBBLOCK // V7X_SUBLANES
    sqrt_half = float(np.float32(np.sqrt(0.5)))
    zero = jnp.zeros((), BF16)
    sub_blocks = eb // EXPERT_SUBBLOCK

    def hidden(sb):
        rows = slice(sb * EXPERT_SUBBLOCK, (sb + 1) * EXPERT_SUBBLOCK)
        hid = jnp.dot(u_ref[rows, :], xtb_ref[...], preferred_element_type=F32)
        for rt in range(tiles_per_sub):
            for lt in range(lane_tiles):
                h_scr[lt, sb * tiles_per_sub + rt] = hid[rt * V7X_SUBLANES:(rt + 1) * V7X_SUBLANES,
                                                        lt * V7X_LANES:(lt + 1) * V7X_LANES]

    def gelu(hid):
        return 0.5 * hid * (1.0 + lax.erf(hid * sqrt_half))

    hidden(0)
    for sb in range(sub_blocks):
        rows = slice(sb * EXPERT_SUBBLOCK, (sb + 1) * EXPERT_SUBBLOCK)
        if sb + 1 < sub_blocks:
            hidden(sb + 1)
        for il in range(sb * keys_per_sub, (sb + 1) * keys_per_sub):
            for lt in range(lane_tiles):
                ls = slice(lt * V7X_LANES, (lt + 1) * V7X_LANES)
                bcast = lambda ref, head: jnp.broadcast_to(ref[head, il:il + 1, ls], (packed, V7X_LANES)).astype(BF16)
                gate = [None] * key_chunks
                for head in range(PEER_HEADS):
                    lrow = bcast(lrow_ref, head)
                    e1 = bcast(e1_ref, head)
                    for rc in range(key_chunks):
                        term = jnp.where(gate_scr[rc, lt, 2 * head] < lrow, gate_scr[rc, lt, 2 * head + 1], zero) * e1
                        gate[rc] = term if gate[rc] is None else gate[rc] + term
                for rc in range(key_chunks):
                    rt = (il * PEER_KEYS + rc * packed) // V7X_SUBLANES
                    act = jnp.concatenate([gelu(h_scr[lt, rt]), gelu(h_scr[lt, rt + 1])], axis=0)
                    w_scr[rt * V7X_SUBLANES:rt * V7X_SUBLANES + packed, ls] = gate[rc] * act.astype(BF16)
        if (sb + 1) % EXPERT_OUT_SUBBLOCKS == 0:
            rows = slice((sb + 1 - EXPERT_OUT_SUBBLOCKS) * EXPERT_SUBBLOCK, (sb + 1) * EXPERT_SUBBLOCK)
            acc_ref[...] += jnp.dot(vt_ref[:, rows], w_scr[rows, :], preferred_element_type=F32)

    @pl.when(step == pl.num_programs(1) - 1)
    def _():
        y = alpha * xt_ref[...] + acc_ref[...]
        mu = jnp.mean(y, axis=0, keepdims=True)
        cen = y - mu
        var = jnp.mean(cen * cen, axis=0, keepdims=True)
        ln = cen * lax.rsqrt(var + LN_EPS) * g_ref[...] + b_ref[...]
        o_ref[...] = ln.T.astype(o_ref.dtype)


def _peer_experts(xt_bf16, xt_f32, u, v_t, rank2, e2, lrow, e1, ln_g, ln_b, alpha, out_dtype):
    d_model, tokens = xt_f32.shape
    experts = u.shape[0]
    tt = EXPERT_TOKENS
    eb = EXPERT_BLOCK
    keys_per_block = eb // PEER_KEYS
    full = pl.BlockSpec((PEER_HEADS, PEER_KEYS, tt), lambda t, e: (0, 0, t))
    rows = pl.BlockSpec((PEER_HEADS, keys_per_block, tt), lambda t, e: (0, e, t))
    col = pl.BlockSpec((d_model, 1), lambda t, e: (0, 0))
    return pl.pallas_call(
        functools.partial(_expert_kernel, alpha=alpha),
        grid=(tokens // tt, experts // eb),
        in_specs=[pl.BlockSpec((d_model, tt), lambda t, e: (0, t)),
                  pl.BlockSpec((d_model, tt), lambda t, e: (0, t)),
                  pl.BlockSpec((eb, d_model), lambda t, e: (e, 0)),
                  pl.BlockSpec((d_model, eb), lambda t, e: (0, e)),
                  full, full, rows, rows, col, col],
        out_specs=pl.BlockSpec((tt, d_model), lambda t, e: (t, 0)),
        out_shape=jax.ShapeDtypeStruct((tokens, d_model), out_dtype),
        scratch_shapes=[pltpu.VMEM((d_model, tt), F32),
                        pltpu.VMEM((tt // V7X_LANES, eb // V7X_SUBLANES, V7X_SUBLANES, V7X_LANES), F32),
                        pltpu.VMEM((eb, tt), BF16),
                        pltpu.VMEM((PEER_KEYS // (2 * V7X_SUBLANES), tt // V7X_LANES, 2 * PEER_HEADS,
                                    2 * V7X_SUBLANES, V7X_LANES), BF16)],
        compiler_params=pltpu.CompilerParams(
            dimension_semantics=("arbitrary", "arbitrary"),
            vmem_limit_bytes=_vmem_limit(
                _nbytes((d_model, tt), BF16) + _nbytes((d_model, tt), F32) + 2 * _nbytes((eb, d_model), BF16)
                + 2 * _nbytes((PEER_HEADS, PEER_KEYS, tt), BF16) + 2 * _nbytes((PEER_HEADS, keys_per_block, tt), F32)
                + 2 * _nbytes((d_model, V7X_LANES), F32) + _nbytes((tt, d_model), F32),
                _nbytes((d_model, tt), F32) + _nbytes((eb, tt), F32) + _nbytes((eb, tt), BF16)
                + 2 * _nbytes((PEER_HEADS, PEER_KEYS, tt), BF16) + 4 * _nbytes((d_model, tt), F32)),
        ),
    )(xt_bf16, xt_f32, u, v_t, rank2, e2, lrow, e1, ln_g, ln_b)


def _lambda_init(layer_idx):
    return 0.8 - 0.6 * math.exp(-0.3 * (layer_idx - 1))


def kernel(x, w_in, ret_gn_g, ret_gn_b, diff_lambda_q1, diff_lambda_k1, diff_lambda_q2, diff_lambda_k2,
           diff_subln_g, w_out, ln1_g, ln1_b, peer_w_query, peer_sub_keys, peer_u, peer_v, ln2_g, ln2_b):
    batch, seq, d_model = x.shape
    depth = w_in.shape[0]
    tokens = batch * seq
    alpha = (2.0 * depth) ** 0.25
    assert w_in.shape[2] == 7 * GROUP and d_model == 2 * GROUP
    assert seq % max(INPROJ_ROWS, RET_ROWS, DIFF_QBLOCK) == 0
    assert tokens % max(OUTPROJ_ROWS, ROUTE_TOKENS, EXPERT_TOKENS) == 0
    assert peer_u.shape[1] == PEER_KEYS * PEER_KEYS and peer_u.shape[1] % EXPERT_BLOCK == 0
    assert peer_sub_keys.shape[1:] == (PEER_HEADS, 2, PEER_KEYS, PEER_KEYS)

    cos_tab, sin_tab = _rope_tables(seq)
    x2d = x.reshape(tokens, d_model)
    for l in range(depth):
        lam_init = _lambda_init(l + 1)
        row = lambda p: p[l].reshape(1, -1)
        rq, rk, rv, rg, dq, dk, dv = _in_projection(x2d, w_in[l].astype(BF16), cos_tab, sin_tab, seq)
        r = _retention(rq, rk, rv, rg, row(ret_gn_g), row(ret_gn_b), batch, seq)
        d = _diff_attention(dq, dk, dv, row(diff_lambda_q1), row(diff_lambda_k1), row(diff_lambda_q2),
                            row(diff_lambda_k2), row(diff_subln_g), batch, seq, lam_init)
        xt_f32, xt_bf16 = _out_projection(r, d, x2d, w_out[l].astype(BF16), row(ln1_g), row(ln1_b), alpha)
        keys = peer_sub_keys[l].reshape(2 * PEER_HEADS, PEER_KEYS, PEER_KEYS).astype(BF16)
        rank2, e2, lrow, e1 = _peer_route(xt_bf16, peer_w_query[l].T.astype(BF16), keys)
        x2d = _peer_experts(xt_bf16, xt_f32, peer_u[l].astype(BF16), peer_v[l].T.astype(BF16),
                            rank2, e2, lrow, e1, ln2_g[l].reshape(-1, 1), ln2_b[l].reshape(-1, 1), alpha, x.dtype)
    return x2d.reshape(batch, seq, d_model)
```

```python
import functools
import math

import jax
import jax.numpy as jnp
import numpy as np
from jax import lax
from jax.experimental import pallas as pl
from jax.experimental.pallas import tpu as pltpu

F32 = jnp.float32
BF16 = jnp.bfloat16

V7X_LANES = 128
V7X_SUBLANES = 8
V7X_VMEM_BYTES = 64 * 1024 * 1024

RET_HEADS = 4
DIFF_HEADS = 4
HEAD_DIM = 128
RET_CHUNK = 128
ROPE_BASE = 10000.0
PEER_HEADS = 8
PEER_KEYS = 128
PEER_TOPK = 16
LN_EPS = 1e-5
GROUP = RET_HEADS * HEAD_DIM

INPROJ_ROWS = 512
RET_ROWS = 512
DIFF_QBLOCK = 256
OUTPROJ_ROWS = 512
ROUTE_TOKENS = 256
EXPERT_TOKENS = 512
EXPERT_BLOCK = 2048
EXPERT_SUBBLOCK = 256


def _vmem_limit(pipelined_bytes, resident_bytes):
    need = 2 * pipelined_bytes + resident_bytes
    return int(min(max(need, 16 * 1024 * 1024), V7X_VMEM_BYTES - 8 * 1024 * 1024))


def _nbytes(shape, dtype):
    return int(np.prod(shape)) * jnp.dtype(dtype).itemsize


def _rope_kernel(cos_ref, sin_ref):
    shape = cos_ref.shape
    half = HEAD_DIM // 2
    pos = lax.broadcasted_iota(jnp.int32, shape, 0).astype(F32)
    lane = lax.broadcasted_iota(jnp.int32, shape, 1)
    freq_idx = jnp.where(lane >= half, lane - half, lane).astype(F32)
    inv = jnp.exp(freq_idx * (-math.log(ROPE_BASE) / half))
    ang = pos * inv
    cos_ref[...] = jnp.cos(ang)
    sin = jnp.sin(ang)
    sin_ref[...] = jnp.where(lane >= half, sin, -sin)


def _rope_tables(seq):
    return pl.pallas_call(
        _rope_kernel,
        out_shape=(jax.ShapeDtypeStruct((seq, HEAD_DIM), F32),) * 2,
    )()


def _inproj_kernel(x_ref, w_ref, cos_ref, sin_ref, rq_ref, rk_ref, rv_ref, rg_ref, dq_ref, dk_ref, dv_ref):
    xb = x_ref[...].astype(BF16)
    cos = cos_ref[...]
    sin = sin_ref[...]

    def proj(group):
        return jnp.dot(xb, w_ref[:, group * GROUP:(group + 1) * GROUP], preferred_element_type=F32)

    def store_rotary(out_ref, p, scale):
        for h in range(RET_HEADS):
            ph = p[:, h * HEAD_DIM:(h + 1) * HEAD_DIM]
            rot = ph * cos + pltpu.roll(ph, HEAD_DIM // 2, axis=1) * sin
            if scale != 1.0:
                rot = rot * scale
            out_ref[:, h * HEAD_DIM:(h + 1) * HEAD_DIM] = rot.astype(out_ref.dtype)

    store_rotary(rq_ref, proj(0), 1.0)
    store_rotary(rk_ref, proj(1), HEAD_DIM ** -0.5)
    rv_ref[...] = proj(2).astype(rv_ref.dtype)
    rg_ref[...] = proj(3).astype(rg_ref.dtype)
    dq_ref[...] = (proj(4) * ((HEAD_DIM // 2) ** -0.5)).astype(dq_ref.dtype)
    dk_ref[...] = proj(5).astype(dk_ref.dtype)
    dv_ref[...] = proj(6).astype(dv_ref.dtype)


def _in_projection(x2d, w_in, cos_tab, sin_tab, seq):
    tokens, d_model = x2d.shape
    cols = w_in.shape[1]
    tm = INPROJ_ROWS
    pos_blocks = seq // tm
    out = jax.ShapeDtypeStruct((tokens, GROUP), BF16)
    group_spec = pl.BlockSpec((tm, GROUP), lambda i: (i, 0))
    tab_spec = pl.BlockSpec((tm, HEAD_DIM), lambda i: (i % pos_blocks, 0))
    return pl.pallas_call(
        _inproj_kernel,
        grid=(tokens // tm,),
        in_specs=[
            pl.BlockSpec((tm, d_model), lambda i: (i, 0)),
            pl.BlockSpec((d_model, cols), lambda i: (0, 0)),
            tab_spec,
            tab_spec,
        ],
        out_specs=[group_spec] * 7,
        out_shape=[out] * 7,
        compiler_params=pltpu.CompilerParams(
            dimension_semantics=("arbitrary",),
            vmem_limit_bytes=_vmem_limit(
                _nbytes((tm, d_model), F32) + _nbytes((d_model, cols), BF16) + 7 * _nbytes((tm, GROUP), BF16)
                + 2 * _nbytes((tm, HEAD_DIM), F32),
                _nbytes((tm, d_model), BF16) + 3 * _nbytes((tm, GROUP), F32)),
        ),
    )(x2d, w_in, cos_tab, sin_tab)


def _retention_kernel(q_ref, k_ref, v_ref, g_ref, gn_g_ref, gn_b_ref, o_ref, state_ref):
    @pl.when(pl.program_id(1) == 0)
    def _():
        state_ref[...] = jnp.zeros_like(state_ref)

    c = RET_CHUNK
    row = lax.broadcasted_iota(jnp.int32, (c, c), 0).astype(F32)
    col = lax.broadcasted_iota(jnp.int32, (c, c), 1).astype(F32)
    rel = row - col
    idx = lax.broadcasted_iota(jnp.int32, (c, 1), 0).astype(F32)
    for h in range(RET_HEADS):
        log_g = math.log1p(-(2.0 ** (-5.0 - h)))
        decay_in = jnp.where(rel >= 0, jnp.exp(jnp.maximum(rel, 0.0) * log_g), 0.0)
        decay_q = jnp.exp((idx + 1.0) * log_g)
        decay_k = jnp.exp((c - 1.0 - idx) * log_g)
        decay_c = math.exp(c * log_g)
        hs = slice(h * HEAD_DIM, (h + 1) * HEAD_DIM)
        gn_g = gn_g_ref[:, hs]
        gn_b = gn_b_ref[:, hs]
        state = state_ref[h]
        for ci in range(q_ref.shape[0] // c):
            rs = slice(ci * c, (ci + 1) * c)
            q = q_ref[rs, hs]
            k = k_ref[rs, hs]
            v = v_ref[rs, hs]
            s = lax.dot_general(q, k, (((1,), (1,)), ((), ())), preferred_element_type=F32) * decay_in
            inner = jnp.dot(s.astype(BF16), v, preferred_element_type=F32)
            cross = jnp.dot(q, state.astype(BF16), preferred_element_type=F32) * decay_q
            kd_t = (k.astype(F32) * decay_k).T.astype(BF16)
            state = state * decay_c + jnp.dot(kd_t, v, preferred_element_type=F32)
            o = inner + cross
            mu = jnp.mean(o, axis=-1, keepdims=True)
            cen = o - mu
            var = jnp.mean(cen * cen, axis=-1, keepdims=True)
            normed = cen * lax.rsqrt(var + LN_EPS) * gn_g + gn_b
            gate = g_ref[rs, hs].astype(F32)
            o_ref[rs, hs] = (gate * jax.nn.sigmoid(gate) * normed).astype(o_ref.dtype)
        state_ref[h] = state


def _retention(rq, rk, rv, rg, gn_g, gn_b, batch, seq):
    rows = RET_ROWS
    blocks = seq // rows
    spec = pl.BlockSpec((rows, GROUP), lambda b, c: (b * blocks + c, 0))
    par = pl.BlockSpec((1, GROUP), lambda b, c: (0, 0))
    return pl.pallas_call(
        _retention_kernel,
        grid=(batch, blocks),
        in_specs=[spec, spec, spec, spec, par, par],
        out_specs=spec,
        out_shape=jax.ShapeDtypeStruct(rq.shape, BF16),
        scratch_shapes=[pltpu.VMEM((RET_HEADS, HEAD_DIM, HEAD_DIM), F32)],
        compiler_params=pltpu.CompilerParams(
            dimension_semantics=("arbitrary", "arbitrary"),
            vmem_limit_bytes=_vmem_limit(5 * _nbytes((rows, GROUP), BF16), 4 * 1024 * 1024),
        ),
    )(rq, rk, rv, rg, gn_g, gn_b)


def _diff_kernel(q_ref, k_ref, v_ref, lq1_ref, lk1_ref, lq2_ref, lk2_ref, g_ref, o_ref, *, lam_init):
    f = lambda r: r[...].astype(F32)
    lam = (jnp.exp(jnp.sum(f(lq1_ref) * f(lk1_ref), axis=-1, keepdims=True))
           - jnp.exp(jnp.sum(f(lq2_ref) * f(lk2_ref), axis=-1, keepdims=True)) + lam_init)
    seq = q_ref.shape[0]
    bq = DIFF_QBLOCK
    half = HEAD_DIM // 2
    lane = lax.broadcasted_iota(jnp.int32, (bq, HEAD_DIM), 1)
    tri = (lax.broadcasted_iota(jnp.int32, (bq, bq), 1) <= lax.broadcasted_iota(jnp.int32, (bq, bq), 0))
    nt = (((1,), (1,)), ((), ()))
    zero = jnp.zeros((), BF16)
    for qi in range(seq // bq):
        past = qi * bq
        q = q_ref[past:past + bq, :]
        q_comp = (jnp.where(lane < half, q, zero), jnp.where(lane >= half, q, zero))
        k_diag = k_ref[past:past + bq, :]
        attn_diag = None
        attn_past = None
        for comp in range(2):
            s_diag = lax.dot_general(q_comp[comp], k_diag, nt, preferred_element_type=F32)
            s_diag = jnp.where(tri, s_diag, -jnp.inf)
            m = jnp.max(s_diag, axis=-1, keepdims=True)
            if past:
                s_past = lax.dot_general(q_comp[comp], k_ref[0:past, :], nt, preferred_element_type=F32)
                m = jnp.maximum(m, jnp.max(s_past, axis=-1, keepdims=True))
                e_past = jnp.exp(s_past - m)
            e_diag = jnp.exp(s_diag - m)
            denom = jnp.sum(e_diag, axis=-1, keepdims=True)
            if past:
                denom = denom + jnp.sum(e_past, axis=-1, keepdims=True)
            weight = (1.0 / denom) if comp == 0 else (-lam / denom)
            attn_diag = e_diag * weight if comp == 0 else attn_diag + e_diag * weight
            if past:
                attn_past = e_past * weight if comp == 0 else attn_past + e_past * weight
        o = jnp.dot(attn_diag.astype(BF16), v_ref[past:past + bq, :], preferred_element_type=F32)
        if past:
            o = o + jnp.dot(attn_past.astype(BF16), v_ref[0:past, :], preferred_element_type=F32)
        normed = o * lax.rsqrt(jnp.mean(o * o, axis=-1, keepdims=True) + LN_EPS)
        o_ref[past:past + bq, :] = (normed * g_ref[...] * (1.0 - lam_init)).astype(o_ref.dtype)


def _diff_attention(dq, dk, dv, lq1, lk1, lq2, lk2, subln_g, batch, seq, lam_init):
    spec = pl.BlockSpec((seq, HEAD_DIM), lambda b, h: (b, h))
    vec = pl.BlockSpec((1, HEAD_DIM // 2), lambda b, h: (0, 0))
    return pl.pallas_call(
        functools.partial(_diff_kernel, lam_init=lam_init),
        grid=(batch, DIFF_HEADS),
        in_specs=[spec, spec, spec, vec, vec, vec, vec, pl.BlockSpec((1, HEAD_DIM), lambda b, h: (0, h))],
        out_specs=spec,
        out_shape=jax.ShapeDtypeStruct(dq.shape, BF16),
        compiler_params=pltpu.CompilerParams(
            dimension_semantics=("arbitrary", "arbitrary"),
            vmem_limit_bytes=_vmem_limit(4 * _nbytes((seq, HEAD_DIM), BF16),
                                         8 * _nbytes((DIFF_QBLOCK, seq), F32)),
        ),
    )(dq, dk, dv, lq1, lk1, lq2, lk2, subln_g)


def _outproj_kernel(r_ref, d_ref, x_ref, w_ref, g_ref, b_ref, xt_ref, xtb_ref, *, alpha):
    mix = (jnp.dot(r_ref[...], w_ref[0:GROUP, :], preferred_element_type=F32)
           + jnp.dot(d_ref[...], w_ref[GROUP:2 * GROUP, :], preferred_element_type=F32))
    y = alpha * x_ref[...] + mix
    mu = jnp.mean(y, axis=-1, keepdims=True)
    cen = y - mu
    var = jnp.mean(cen * cen, axis=-1, keepdims=True)
    ln = cen * lax.rsqrt(var + LN_EPS) * g_ref[...] + b_ref[...]
    ln_t = ln.T
    xt_ref[...] = ln_t
    xtb_ref[...] = ln_t.astype(xtb_ref.dtype)


def _out_projection(r, d, x2d, w_out, ln_g, ln_b, alpha):
    tokens, d_model = x2d.shape
    tm = OUTPROJ_ROWS
    half = pl.BlockSpec((tm, GROUP), lambda i: (i, 0))
    par = pl.BlockSpec((1, d_model), lambda i: (0, 0))
    t_spec = pl.BlockSpec((d_model, tm), lambda i: (0, i))
    return pl.pallas_call(
        functools.partial(_outproj_kernel, alpha=alpha),
        grid=(tokens // tm,),
        in_specs=[half, half, pl.BlockSpec((tm, d_model), lambda i: (i, 0)),
                  pl.BlockSpec((2 * GROUP, d_model), lambda i: (0, 0)), par, par],
        out_specs=[t_spec, t_spec],
        out_shape=[jax.ShapeDtypeStruct((d_model, tokens), F32), jax.ShapeDtypeStruct((d_model, tokens), BF16)],
        compiler_params=pltpu.CompilerParams(
            dimension_semantics=("arbitrary",),
            vmem_limit_bytes=_vmem_limit(
                2 * _nbytes((tm, GROUP), BF16) + _nbytes((tm, d_model), F32) + _nbytes((2 * GROUP, d_model), BF16)
                + _nbytes((d_model, tm), F32) + _nbytes((d_model, tm), BF16),
                4 * _nbytes((tm, d_model), F32)),
        ),
    )(r, d, x2d, w_out, ln_g, ln_b)


def _candidate_cells():
    return [(p, q) for p in range(PEER_TOPK) for q in range(PEER_TOPK) if (p + 1) * (q + 1) <= PEER_TOPK]


def _route_kernel(xt_ref, wq_ref, keys_ref, rank2_ref, e2_ref, lrow_ref, e1_ref,
                  q_scr, score_scr, work_scr, top_scr, idx_scr, len_scr, invz_scr):
    tt = xt_ref.shape[1]
    k = PEER_TOPK
    sub = V7X_SUBLANES
    groups = PEER_KEYS // sub
    lane_tiles = tt // V7X_LANES
    q_scr[...] = jnp.dot(wq_ref[...], xt_ref[...], preferred_element_type=F32).astype(BF16)
    group_iota = [(lax.broadcasted_iota(jnp.int32, (sub, V7X_LANES), 0) + g * sub).astype(F32) for g in range(groups)]

    chains = [(slot, lt) for slot in range(2 * PEER_HEADS) for lt in range(lane_tiles)]
    for slot in range(2 * PEER_HEADS):
        src = (slot % PEER_HEADS) * 2 + slot // PEER_HEADS
        scores = jnp.dot(keys_ref[src], q_scr[src * PEER_KEYS:(src + 1) * PEER_KEYS, :],
                         preferred_element_type=F32)
        score_scr[slot] = scores
        for lt in range(lane_tiles):
            for g in range(groups):
                work_scr[chains.index((slot, lt)), g] = scores[g * sub:(g + 1) * sub,
                                                               lt * V7X_LANES:(lt + 1) * V7X_LANES]

    def extract(r, carry):
        for c, (slot, lt) in enumerate(chains):
            ls = slice(lt * V7X_LANES, (lt + 1) * V7X_LANES)
            cur = [work_scr[c, g] for g in range(groups)]
            vals, idxs = list(cur), list(group_iota)
            while len(vals) > 1:
                left = [vals[a] >= vals[a + 1] for a in range(0, len(vals), 2)]
                idxs = [jnp.where(left[a // 2], idxs[a], idxs[a + 1]) for a in range(0, len(vals), 2)]
                vals = [jnp.maximum(vals[a], vals[a + 1]) for a in range(0, len(vals), 2)]
            v, i = vals[0], idxs[0]
            for shift in (4, 2, 1):
                v_other = pltpu.roll(v, shift, axis=0)
                i_other = pltpu.roll(i, shift, axis=0)
                keep = (v > v_other) | ((v == v_other) & (i < i_other))
                v = jnp.where(keep, v, v_other)
                i = jnp.where(keep, i, i_other)
            top_scr[r, slot:slot + 1, ls] = v[0:1, :]
            idx_scr[r, slot:slot + 1, ls] = i[0:1, :]
            for g in range(groups):
                work_scr[c, g] = jnp.where(group_iota[g] == i, -jnp.inf, cur[g])
        return carry

    lax.fori_loop(0, k, extract, 0)

    cells = _candidate_cells()
    a = [top_scr[p, 0:PEER_HEADS, :] for p in range(k)]
    b = [top_scr[q, PEER_HEADS:2 * PEER_HEADS, :] for q in range(k)]
    total = {cell: a[cell[0]] + b[cell[1]] for cell in cells}
    ahead = {cell: jnp.full((PEER_HEADS, tt), float((cell[0] + 1) * (cell[1] + 1) - 1), F32) for cell in cells}
    for xi, x in enumerate(cells):
        for y in cells[xi + 1:]:
            if (x[0] <= y[0] and x[1] <= y[1]) or (x[0] >= y[0] and x[1] >= y[1]):
                continue
            x_first = total[x] >= total[y]
            ahead[y] = ahead[y] + jnp.where(x_first, 1.0, 0.0)
            ahead[x] = ahead[x] + jnp.where(x_first, 0.0, 1.0)
    best = total[(0, 0)]
    denom = jnp.zeros((PEER_HEADS, tt), F32)
    counts = [jnp.zeros((PEER_HEADS, tt), F32) for _ in range(k)]
    for cell in cells:
        chosen = ahead[cell] < float(k)
        denom = denom + jnp.where(chosen, jnp.exp(total[cell] - best), 0.0)
        counts[cell[0]] = counts[cell[0]] + jnp.where(chosen, 1.0, 0.0)
    invz_scr[...] = 1.0 / denom
    for p in range(k):
        len_scr[p] = counts[p]

    key_iota = lax.broadcasted_iota(jnp.int32, (PEER_KEYS, tt), 0).astype(F32)
    for head in range(PEER_HEADS):
        hs = slice(head, head + 1)
        other = PEER_HEADS + head
        os_ = slice(other, other + 1)
        lrow = jnp.zeros((PEER_KEYS, tt), F32)
        rank2 = jnp.full((PEER_KEYS, tt), float(k), F32)
        for r in range(k):
            lrow = jnp.where(key_iota == idx_scr[r, hs, :], len_scr[r, hs, :], lrow)
            rank2 = jnp.where(key_iota == idx_scr[r, os_, :], float(r), rank2)
        lrow_ref[head] = lrow
        rank2_ref[head] = rank2.astype(rank2_ref.dtype)
        e1_ref[head] = jnp.exp(score_scr[head] - top_scr[0, hs, :]) * invz_scr[hs, :]
        e2_ref[head] = jnp.exp(score_scr[other] - top_scr[0, os_, :]).astype(e2_ref.dtype)


def _peer_route(xt_bf16, wq_t, keys):
    d_model, tokens = xt_bf16.shape
    tt = ROUTE_TOKENS
    qdim = wq_t.shape[0]
    slots = 2 * PEER_HEADS
    shape = (PEER_HEADS, PEER_KEYS, tokens)
    out_spec = pl.BlockSpec((PEER_HEADS, PEER_KEYS, tt), lambda t: (0, 0, t))
    return pl.pallas_call(
        _route_kernel,
        grid=(tokens // tt,),
        in_specs=[pl.BlockSpec((d_model, tt), lambda t: (0, t)),
                  pl.BlockSpec((qdim, d_model), lambda t: (0, 0)),
                  pl.BlockSpec((slots, PEER_KEYS, PEER_KEYS), lambda t: (0, 0, 0))],
        out_specs=[out_spec] * 4,
        out_shape=[jax.ShapeDtypeStruct(shape, BF16), jax.ShapeDtypeStruct(shape, BF16),
                   jax.ShapeDtypeStruct(shape, F32), jax.ShapeDtypeStruct(shape, F32)],
        scratch_shapes=[
            pltpu.VMEM((qdim, tt), BF16),
            pltpu.VMEM((slots, PEER_KEYS, tt), F32),
            pltpu.VMEM((slots * (tt // V7X_LANES), PEER_KEYS // V7X_SUBLANES, V7X_SUBLANES, V7X_LANES), F32),
            pltpu.VMEM((PEER_TOPK, slots, tt), F32),
            pltpu.VMEM((PEER_TOPK, slots, tt), F32),
            pltpu.VMEM((PEER_TOPK, PEER_HEADS, tt), F32),
            pltpu.VMEM((PEER_HEADS, tt), F32),
        ],
        compiler_params=pltpu.CompilerParams(
            dimension_semantics=("arbitrary",),
            vmem_limit_bytes=_vmem_limit(
                _nbytes((d_model, tt), BF16) + _nbytes((qdim, d_model), BF16)
                + _nbytes((slots, PEER_KEYS, PEER_KEYS), BF16) + 3 * _nbytes((PEER_HEADS, PEER_KEYS, tt), F32),
                _nbytes((qdim, tt), F32) + 4 * _nbytes((slots, PEER_KEYS, tt), F32)),
        ),
    )(xt_bf16, wq_t, keys)


def _expert_kernel(xtb_ref, xt_ref, u_ref, vt_ref, rank2_ref, e2_ref, lrow_ref, e1_ref, g_ref, b_ref,
                   o_ref, acc_ref, w_scr, gate_scr, *, alpha):
    step = pl.program_id(1)
    eb, tt = w_scr.shape
    lane_tiles = tt // V7X_LANES
    packed = 2 * V7X_SUBLANES
    key_chunks = PEER_KEYS // packed

    @pl.when(step == 0)
    def _():
        acc_ref[...] = jnp.zeros_like(acc_ref)
        for rc in range(key_chunks):
            for lt in range(lane_tiles):
                for head in range(PEER_HEADS):
                    src = (head, slice(rc * packed, (rc + 1) * packed), slice(lt * V7X_LANES, (lt + 1) * V7X_LANES))
                    gate_scr[rc, lt, 2 * head] = rank2_ref[src]
                    gate_scr[rc, lt, 2 * head + 1] = e2_ref[src]

    keys_per_sub = EXPERT_SUBBLOCK // PEER_KEYS
    sqrt_half = float(np.float32(np.sqrt(0.5)))
    zero = jnp.zeros((), BF16)
    sub_blocks = eb // EXPERT_SUBBLOCK

    def hidden(sb):
        rows = slice(sb * EXPERT_SUBBLOCK, (sb + 1) * EXPERT_SUBBLOCK)
        hid = jnp.dot(u_ref[rows, :], xtb_ref[...], preferred_element_type=F32)
        w_scr[rows, :] = (0.5 * hid * (1.0 + lax.erf(hid * sqrt_half))).astype(BF16)

    hidden(0)
    for sb in range(sub_blocks):
        rows = slice(sb * EXPERT_SUBBLOCK, (sb + 1) * EXPERT_SUBBLOCK)
        if sb + 1 < sub_blocks:
            hidden(sb + 1)
        for il in range(sb * keys_per_sub, (sb + 1) * keys_per_sub):
            for lt in range(lane_tiles):
                ls = slice(lt * V7X_LANES, (lt + 1) * V7X_LANES)
                bcast = lambda ref, head: jnp.broadcast_to(ref[head, il:il + 1, ls], (packed, V7X_LANES)).astype(BF16)
                gate = [None] * key_chunks
                for head in range(PEER_HEADS):
                    lrow = bcast(lrow_ref, head)
                    e1 = bcast(e1_ref, head)
                    for rc in range(key_chunks):
                        term = jnp.where(gate_scr[rc, lt, 2 * head] < lrow, gate_scr[rc, lt, 2 * head + 1], zero) * e1
                        gate[rc] = term if gate[rc] is None else gate[rc] + term
                for rc in range(key_chunks):
                    ns = slice(il * PEER_KEYS + rc * packed, il * PEER_KEYS + (rc + 1) * packed)
                    w_scr[ns, ls] = gate[rc] * w_scr[ns, ls]
        acc_ref[...] += jnp.dot(vt_ref[:, rows], w_scr[rows, :], preferred_element_type=F32)

    @pl.when(step == pl.num_programs(1) - 1)
    def _():
        y = alpha * xt_ref[...] + acc_ref[...]
        mu = jnp.mean(y, axis=0, keepdims=True)
        cen = y - mu
        var = jnp.mean(cen * cen, axis=0, keepdims=True)
        ln = cen * lax.rsqrt(var + LN_EPS) * g_ref[...] + b_ref[...]
        o_ref[...] = ln.T.astype(o_ref.dtype)


def _peer_experts(xt_bf16, xt_f32, u, v_t, rank2, e2, lrow, e1, ln_g, ln_b, alpha, out_dtype):
    d_model, tokens = xt_f32.shape
    experts = u.shape[0]
    tt = EXPERT_TOKENS
    eb = EXPERT_BLOCK
    keys_per_block = eb // PEER_KEYS
    full = pl.BlockSpec((PEER_HEADS, PEER_KEYS, tt), lambda t, e: (0, 0, t))
    rows = pl.BlockSpec((PEER_HEADS, keys_per_block, tt), lambda t, e: (0, e, t))
    col = pl.BlockSpec((d_model, 1), lambda t, e: (0, 0))
    return pl.pallas_call(
        functools.partial(_expert_kernel, alpha=alpha),
        grid=(tokens // tt, experts // eb),
        in_specs=[pl.BlockSpec((d_model, tt), lambda t, e: (0, t)),
                  pl.BlockSpec((d_model, tt), lambda t, e: (0, t)),
                  pl.BlockSpec((eb, d_model), lambda t, e: (e, 0)),
                  pl.BlockSpec((d_model, eb), lambda t, e: (0, e)),
                  full, full, rows, rows, col, col],
        out_specs=pl.BlockSpec((tt, d_model), lambda t, e: (t, 0)),
        out_shape=jax.ShapeDtypeStruct((tokens, d_model), out_dtype),
        scratch_shapes=[pltpu.VMEM((d_model, tt), F32),
                        pltpu.VMEM((eb, tt), BF16),
                        pltpu.VMEM((PEER_KEYS // (2 * V7X_SUBLANES), tt // V7X_LANES, 2 * PEER_HEADS,
                                    2 * V7X_SUBLANES, V7X_LANES), BF16)],
        compiler_params=pltpu.CompilerParams(
            dimension_semantics=("arbitrary", "arbitrary"),
            vmem_limit_bytes=_vmem_limit(
                _nbytes((d_model, tt), BF16) + _nbytes((d_model, tt), F32) + 2 * _nbytes((eb, d_model), BF16)
                + 2 * _nbytes((PEER_HEADS, PEER_KEYS, tt), BF16) + 2 * _nbytes((PEER_HEADS, keys_per_block, tt), F32)
                + 2 * _nbytes((d_model, V7X_LANES), F32) + _nbytes((tt, d_model), F32),
                _nbytes((d_model, tt), F32) + _nbytes((eb, tt), BF16) + _nbytes((EXPERT_SUBBLOCK, tt), F32)
                + 2 * _nbytes((PEER_HEADS, PEER_KEYS, tt), BF16) + 4 * _nbytes((d_model, tt), F32)),
        ),
    )(xt_bf16, xt_f32, u, v_t, rank2, e2, lrow, e1, ln_g, ln_b)


def _lambda_init(layer_idx):
    return 0.8 - 0.6 * math.exp(-0.3 * (layer_idx - 1))


def kernel(x, w_in, ret_gn_g, ret_gn_b, diff_lambda_q1, diff_lambda_k1, diff_lambda_q2, diff_lambda_k2,
           diff_subln_g, w_out, ln1_g, ln1_b, peer_w_query, peer_sub_keys, peer_u, peer_v, ln2_g, ln2_b):
    batch, seq, d_model = x.shape
    depth = w_in.shape[0]
    tokens = batch * seq
    alpha = (2.0 * depth) ** 0.25
    assert w_in.shape[2] == 7 * GROUP and d_model == 2 * GROUP
    assert seq % max(INPROJ_ROWS, RET_ROWS, DIFF_QBLOCK) == 0
    assert tokens % max(OUTPROJ_ROWS, ROUTE_TOKENS, EXPERT_TOKENS) == 0
    assert peer_u.shape[1] == PEER_KEYS * PEER_KEYS and peer_u.shape[1] % EXPERT_BLOCK == 0
    assert peer_sub_keys.shape[1:] == (PEER_HEADS, 2, PEER_KEYS, PEER_KEYS)

    cos_tab, sin_tab = _rope_tables(seq)
    x2d = x.reshape(tokens, d_model)
    for l in range(depth):
        lam_init = _lambda_init(l + 1)
        row = lambda p: p[l].reshape(1, -1)
        rq, rk, rv, rg, dq, dk, dv = _in_projection(x2d, w_in[l].astype(BF16), cos_tab, sin_tab, seq)
        r = _retention(rq, rk, rv, rg, row(ret_gn_g), row(ret_gn_b), batch, seq)
        d = _diff_attention(dq, dk, dv, row(diff_lambda_q1), row(diff_lambda_k1), row(diff_lambda_q2),
                            row(diff_lambda_k2), row(diff_subln_g), batch, seq, lam_init)
        xt_f32, xt_bf16 = _out_projection(r, d, x2d, w_out[l].astype(BF16), row(ln1_g), row(ln1_b), alpha)
        keys = peer_sub_keys[l].reshape(2 * PEER_HEADS, PEER_KEYS, PEER_KEYS).astype(BF16)
        rank2, e2, lrow, e1 = _peer_route(xt_bf16, peer_w_query[l].T.astype(BF16), keys)
        x2d = _peer_experts(xt_bf16, xt_f32, peer_u[l].astype(BF16), peer_v[l].T.astype(BF16),
                            rank2, e2, lrow, e1, ln2_g[l].reshape(-1, 1), ln2_b[l].reshape(-1, 1), alpha, x.dtype)
    return x2d.reshape(batch, seq, d_model)
```

```python
import functools
import math

import jax
import jax.numpy as jnp
import numpy as np
from jax import lax
from jax.experimental import pallas as pl
from jax.experimental.pallas import tpu as pltpu

F32 = jnp.float32
BF16 = jnp.bfloat16

V7X_LANES = 128
V7X_SUBLANES = 8
V7X_VMEM_BYTES = 64 * 1024 * 1024

RET_HEADS = 4
DIFF_HEADS = 4
HEAD_DIM = 128
RET_CHUNK = 128
ROPE_BASE = 10000.0
PEER_HEADS = 8
PEER_KEYS = 128
PEER_TOPK = 16
LN_EPS = 1e-5
GROUP = RET_HEADS * HEAD_DIM

INPROJ_ROWS = 512
RET_ROWS = 512
DIFF_QBLOCK = 256
OUTPROJ_ROWS = 512
ROUTE_TOKENS = 256
EXPERT_TOKENS = 512
EXPERT_BLOCK = 2048
EXPERT_HIDDEN_ROWS = 512


def _vmem_limit(pipelined_bytes, resident_bytes):
    need = 2 * pipelined_bytes + resident_bytes
    return int(min(max(need, 16 * 1024 * 1024), V7X_VMEM_BYTES - 8 * 1024 * 1024))


def _nbytes(shape, dtype):
    return int(np.prod(shape)) * jnp.dtype(dtype).itemsize


def _rope_kernel(cos_ref, sin_ref):
    shape = cos_ref.shape
    half = HEAD_DIM // 2
    pos = lax.broadcasted_iota(jnp.int32, shape, 0).astype(F32)
    lane = lax.broadcasted_iota(jnp.int32, shape, 1)
    freq_idx = jnp.where(lane >= half, lane - half, lane).astype(F32)
    inv = jnp.exp(freq_idx * (-math.log(ROPE_BASE) / half))
    ang = pos * inv
    cos_ref[...] = jnp.cos(ang)
    sin = jnp.sin(ang)
    sin_ref[...] = jnp.where(lane >= half, sin, -sin)


def _rope_tables(seq):
    return pl.pallas_call(
        _rope_kernel,
        out_shape=(jax.ShapeDtypeStruct((seq, HEAD_DIM), F32),) * 2,
    )()


def _inproj_kernel(x_ref, w_ref, cos_ref, sin_ref, rq_ref, rk_ref, rv_ref, rg_ref, dq_ref, dk_ref, dv_ref):
    xb = x_ref[...].astype(BF16)
    cos = cos_ref[...]
    sin = sin_ref[...]

    def proj(group):
        return jnp.dot(xb, w_ref[:, group * GROUP:(group + 1) * GROUP], preferred_element_type=F32)

    def store_rotary(out_ref, p, scale):
        for h in range(RET_HEADS):
            ph = p[:, h * HEAD_DIM:(h + 1) * HEAD_DIM]
            rot = ph * cos + pltpu.roll(ph, HEAD_DIM // 2, axis=1) * sin
            if scale != 1.0:
                rot = rot * scale
            out_ref[:, h * HEAD_DIM:(h + 1) * HEAD_DIM] = rot.astype(out_ref.dtype)

    store_rotary(rq_ref, proj(0), 1.0)
    store_rotary(rk_ref, proj(1), HEAD_DIM ** -0.5)
    rv_ref[...] = proj(2).astype(rv_ref.dtype)
    rg_ref[...] = proj(3).astype(rg_ref.dtype)
    dq_ref[...] = (proj(4) * ((HEAD_DIM // 2) ** -0.5)).astype(dq_ref.dtype)
    dk_ref[...] = proj(5).astype(dk_ref.dtype)
    dv_ref[...] = proj(6).astype(dv_ref.dtype)


def _in_projection(x2d, w_in, cos_tab, sin_tab, seq):
    tokens, d_model = x2d.shape
    cols = w_in.shape[1]
    tm = INPROJ_ROWS
    pos_blocks = seq // tm
    out = jax.ShapeDtypeStruct((tokens, GROUP), BF16)
    group_spec = pl.BlockSpec((tm, GROUP), lambda i: (i, 0))
    tab_spec = pl.BlockSpec((tm, HEAD_DIM), lambda i: (i % pos_blocks, 0))
    return pl.pallas_call(
        _inproj_kernel,
        grid=(tokens // tm,),
        in_specs=[
            pl.BlockSpec((tm, d_model), lambda i: (i, 0)),
            pl.BlockSpec((d_model, cols), lambda i: (0, 0)),
            tab_spec,
            tab_spec,
        ],
        out_specs=[group_spec] * 7,
        out_shape=[out] * 7,
        compiler_params=pltpu.CompilerParams(
            dimension_semantics=("arbitrary",),
            vmem_limit_bytes=_vmem_limit(
                _nbytes((tm, d_model), F32) + _nbytes((d_model, cols), BF16) + 7 * _nbytes((tm, GROUP), BF16)
                + 2 * _nbytes((tm, HEAD_DIM), F32),
                _nbytes((tm, d_model), BF16) + 3 * _nbytes((tm, GROUP), F32)),
        ),
    )(x2d, w_in, cos_tab, sin_tab)


def _retention_kernel(q_ref, k_ref, v_ref, g_ref, gn_g_ref, gn_b_ref, o_ref, state_ref):
    @pl.when(pl.program_id(1) == 0)
    def _():
        state_ref[...] = jnp.zeros_like(state_ref)

    c = RET_CHUNK
    row = lax.broadcasted_iota(jnp.int32, (c, c), 0).astype(F32)
    col = lax.broadcasted_iota(jnp.int32, (c, c), 1).astype(F32)
    rel = row - col
    idx = lax.broadcasted_iota(jnp.int32, (c, 1), 0).astype(F32)
    for h in range(RET_HEADS):
        log_g = math.log1p(-(2.0 ** (-5.0 - h)))
        decay_in = jnp.where(rel >= 0, jnp.exp(jnp.maximum(rel, 0.0) * log_g), 0.0)
        decay_q = jnp.exp((idx + 1.0) * log_g)
        decay_k = jnp.exp((c - 1.0 - idx) * log_g)
        decay_c = math.exp(c * log_g)
        hs = slice(h * HEAD_DIM, (h + 1) * HEAD_DIM)
        gn_g = gn_g_ref[:, hs]
        gn_b = gn_b_ref[:, hs]
        state = state_ref[h]
        for ci in range(q_ref.shape[0] // c):
            rs = slice(ci * c, (ci + 1) * c)
            q = q_ref[rs, hs]
            k = k_ref[rs, hs]
            v = v_ref[rs, hs]
            s = lax.dot_general(q, k, (((1,), (1,)), ((), ())), preferred_element_type=F32) * decay_in
            inner = jnp.dot(s.astype(BF16), v, preferred_element_type=F32)
            cross = jnp.dot(q, state.astype(BF16), preferred_element_type=F32) * decay_q
            kd_t = (k.astype(F32) * decay_k).T.astype(BF16)
            state = state * decay_c + jnp.dot(kd_t, v, preferred_element_type=F32)
            o = inner + cross
            mu = jnp.mean(o, axis=-1, keepdims=True)
            cen = o - mu
            var = jnp.mean(cen * cen, axis=-1, keepdims=True)
            normed = cen * lax.rsqrt(var + LN_EPS) * gn_g + gn_b
            gate = g_ref[rs, hs].astype(F32)
            o_ref[rs, hs] = (gate * jax.nn.sigmoid(gate) * normed).astype(o_ref.dtype)
        state_ref[h] = state


def _retention(rq, rk, rv, rg, gn_g, gn_b, batch, seq):
    rows = RET_ROWS
    blocks = seq // rows
    spec = pl.BlockSpec((rows, GROUP), lambda b, c: (b * blocks + c, 0))
    par = pl.BlockSpec((1, GROUP), lambda b, c: (0, 0))
    return pl.pallas_call(
        _retention_kernel,
        grid=(batch, blocks),
        in_specs=[spec, spec, spec, spec, par, par],
        out_specs=spec,
        out_shape=jax.ShapeDtypeStruct(rq.shape, BF16),
        scratch_shapes=[pltpu.VMEM((RET_HEADS, HEAD_DIM, HEAD_DIM), F32)],
        compiler_params=pltpu.CompilerParams(
            dimension_semantics=("arbitrary", "arbitrary"),
            vmem_limit_bytes=_vmem_limit(5 * _nbytes((rows, GROUP), BF16), 4 * 1024 * 1024),
        ),
    )(rq, rk, rv, rg, gn_g, gn_b)


def _diff_kernel(q_ref, k_ref, v_ref, lq1_ref, lk1_ref, lq2_ref, lk2_ref, g_ref, o_ref, *, lam_init):
    f = lambda r: r[...].astype(F32)
    lam = (jnp.exp(jnp.sum(f(lq1_ref) * f(lk1_ref), axis=-1, keepdims=True))
           - jnp.exp(jnp.sum(f(lq2_ref) * f(lk2_ref), axis=-1, keepdims=True)) + lam_init)
    seq = q_ref.shape[0]
    bq = DIFF_QBLOCK
    half = HEAD_DIM // 2
    lane = lax.broadcasted_iota(jnp.int32, (bq, HEAD_DIM), 1)
    tri = (lax.broadcasted_iota(jnp.int32, (bq, bq), 1) <= lax.broadcasted_iota(jnp.int32, (bq, bq), 0))
    nt = (((1,), (1,)), ((), ()))
    zero = jnp.zeros((), BF16)
    for qi in range(seq // bq):
        past = qi * bq
        q = q_ref[past:past + bq, :]
        q_comp = (jnp.where(lane < half, q, zero), jnp.where(lane >= half, q, zero))
        k_diag = k_ref[past:past + bq, :]
        attn_diag = None
        attn_past = None
        for comp in range(2):
            s_diag = lax.dot_general(q_comp[comp], k_diag, nt, preferred_element_type=F32)
            s_diag = jnp.where(tri, s_diag, -jnp.inf)
            m = jnp.max(s_diag, axis=-1, keepdims=True)
            if past:
                s_past = lax.dot_general(q_comp[comp], k_ref[0:past, :], nt, preferred_element_type=F32)
                m = jnp.maximum(m, jnp.max(s_past, axis=-1, keepdims=True))
                e_past = jnp.exp(s_past - m)
            e_diag = jnp.exp(s_diag - m)
            denom = jnp.sum(e_diag, axis=-1, keepdims=True)
            if past:
                denom = denom + jnp.sum(e_past, axis=-1, keepdims=True)
            weight = (1.0 / denom) if comp == 0 else (-lam / denom)
            attn_diag = e_diag * weight if comp == 0 else attn_diag + e_diag * weight
            if past:
                attn_past = e_past * weight if comp == 0 else attn_past + e_past * weight
        o = jnp.dot(attn_diag.astype(BF16), v_ref[past:past + bq, :], preferred_element_type=F32)
        if past:
            o = o + jnp.dot(attn_past.astype(BF16), v_ref[0:past, :], preferred_element_type=F32)
        normed = o * lax.rsqrt(jnp.mean(o * o, axis=-1, keepdims=True) + LN_EPS)
        o_ref[past:past + bq, :] = (normed * g_ref[...] * (1.0 - lam_init)).astype(o_ref.dtype)


def _diff_attention(dq, dk, dv, lq1, lk1, lq2, lk2, subln_g, batch, seq, lam_init):
    spec = pl.BlockSpec((seq, HEAD_DIM), lambda b, h: (b, h))
    vec = pl.BlockSpec((1, HEAD_DIM // 2), lambda b, h: (0, 0))
    return pl.pallas_call(
        functools.partial(_diff_kernel, lam_init=lam_init),
        grid=(batch, DIFF_HEADS),
        in_specs=[spec, spec, spec, vec, vec, vec, vec, pl.BlockSpec((1, HEAD_DIM), lambda b, h: (0, h))],
        out_specs=spec,
        out_shape=jax.ShapeDtypeStruct(dq.shape, BF16),
        compiler_params=pltpu.CompilerParams(
            dimension_semantics=("arbitrary", "arbitrary"),
            vmem_limit_bytes=_vmem_limit(4 * _nbytes((seq, HEAD_DIM), BF16),
                                         8 * _nbytes((DIFF_QBLOCK, seq), F32)),
        ),
    )(dq, dk, dv, lq1, lk1, lq2, lk2, subln_g)


def _outproj_kernel(r_ref, d_ref, x_ref, w_ref, g_ref, b_ref, xt_ref, xtb_ref, *, alpha):
    mix = (jnp.dot(r_ref[...], w_ref[0:GROUP, :], preferred_element_type=F32)
           + jnp.dot(d_ref[...], w_ref[GROUP:2 * GROUP, :], preferred_element_type=F32))
    y = alpha * x_ref[...] + mix
    mu = jnp.mean(y, axis=-1, keepdims=True)
    cen = y - mu
    var = jnp.mean(cen * cen, axis=-1, keepdims=True)
    ln = cen * lax.rsqrt(var + LN_EPS) * g_ref[...] + b_ref[...]
    ln_t = ln.T
    xt_ref[...] = ln_t
    xtb_ref[...] = ln_t.astype(xtb_ref.dtype)


def _out_projection(r, d, x2d, w_out, ln_g, ln_b, alpha):
    tokens, d_model = x2d.shape
    tm = OUTPROJ_ROWS
    half = pl.BlockSpec((tm, GROUP), lambda i: (i, 0))
    par = pl.BlockSpec((1, d_model), lambda i: (0, 0))
    t_spec = pl.BlockSpec((d_model, tm), lambda i: (0, i))
    return pl.pallas_call(
        functools.partial(_outproj_kernel, alpha=alpha),
        grid=(tokens // tm,),
        in_specs=[half, half, pl.BlockSpec((tm, d_model), lambda i: (i, 0)),
                  pl.BlockSpec((2 * GROUP, d_model), lambda i: (0, 0)), par, par],
        out_specs=[t_spec, t_spec],
        out_shape=[jax.ShapeDtypeStruct((d_model, tokens), F32), jax.ShapeDtypeStruct((d_model, tokens), BF16)],
        compiler_params=pltpu.CompilerParams(
            dimension_semantics=("arbitrary",),
            vmem_limit_bytes=_vmem_limit(
                2 * _nbytes((tm, GROUP), BF16) + _nbytes((tm, d_model), F32) + _nbytes((2 * GROUP, d_model), BF16)
                + _nbytes((d_model, tm), F32) + _nbytes((d_model, tm), BF16),
                4 * _nbytes((tm, d_model), F32)),
        ),
    )(r, d, x2d, w_out, ln_g, ln_b)


def _candidate_cells():
    return [(p, q) for p in range(PEER_TOPK) for q in range(PEER_TOPK) if (p + 1) * (q + 1) <= PEER_TOPK]


def _route_kernel(xt_ref, wq_ref, keys_ref, rank2_ref, e2_ref, lrow_ref, e1_ref,
                  q_scr, score_scr, work_scr, top_scr, idx_scr, len_scr, invz_scr):
    tt = xt_ref.shape[1]
    k = PEER_TOPK
    sub = V7X_SUBLANES
    groups = PEER_KEYS // sub
    lane_tiles = tt // V7X_LANES
    q_scr[...] = jnp.dot(wq_ref[...], xt_ref[...], preferred_element_type=F32).astype(BF16)
    group_iota = [(lax.broadcasted_iota(jnp.int32, (sub, V7X_LANES), 0) + g * sub).astype(F32) for g in range(groups)]

    chains = [(slot, lt) for slot in range(2 * PEER_HEADS) for lt in range(lane_tiles)]
    for slot in range(2 * PEER_HEADS):
        src = (slot % PEER_HEADS) * 2 + slot // PEER_HEADS
        scores = jnp.dot(keys_ref[src], q_scr[src * PEER_KEYS:(src + 1) * PEER_KEYS, :],
                         preferred_element_type=F32)
        score_scr[slot] = scores
        for lt in range(lane_tiles):
            for g in range(groups):
                work_scr[chains.index((slot, lt)), g] = scores[g * sub:(g + 1) * sub,
                                                               lt * V7X_LANES:(lt + 1) * V7X_LANES]

    def extract(r, carry):
        for c, (slot, lt) in enumerate(chains):
            ls = slice(lt * V7X_LANES, (lt + 1) * V7X_LANES)
            cur = [work_scr[c, g] for g in range(groups)]
            vals, idxs = list(cur), list(group_iota)
            while len(vals) > 1:
                left = [vals[a] >= vals[a + 1] for a in range(0, len(vals), 2)]
                idxs = [jnp.where(left[a // 2], idxs[a], idxs[a + 1]) for a in range(0, len(vals), 2)]
                vals = [jnp.maximum(vals[a], vals[a + 1]) for a in range(0, len(vals), 2)]
            v, i = vals[0], idxs[0]
            for shift in (4, 2, 1):
                v_other = pltpu.roll(v, shift, axis=0)
                i_other = pltpu.roll(i, shift, axis=0)
                keep = (v > v_other) | ((v == v_other) & (i < i_other))
                v = jnp.where(keep, v, v_other)
                i = jnp.where(keep, i, i_other)
            top_scr[r, slot:slot + 1, ls] = v[0:1, :]
            idx_scr[r, slot:slot + 1, ls] = i[0:1, :]
            for g in range(groups):
                work_scr[c, g] = jnp.where(group_iota[g] == i, -jnp.inf, cur[g])
        return carry

    lax.fori_loop(0, k, extract, 0)

    cells = _candidate_cells()
    a = [top_scr[p, 0:PEER_HEADS, :] for p in range(k)]
    b = [top_scr[q, PEER_HEADS:2 * PEER_HEADS, :] for q in range(k)]
    total = {cell: a[cell[0]] + b[cell[1]] for cell in cells}
    ahead = {cell: jnp.full((PEER_HEADS, tt), float((cell[0] + 1) * (cell[1] + 1) - 1), F32) for cell in cells}
    for xi, x in enumerate(cells):
        for y in cells[xi + 1:]:
            if (x[0] <= y[0] and x[1] <= y[1]) or (x[0] >= y[0] and x[1] >= y[1]):
                continue
            x_first = total[x] >= total[y]
            ahead[y] = ahead[y] + jnp.where(x_first, 1.0, 0.0)
            ahead[x] = ahead[x] + jnp.where(x_first, 0.0, 1.0)
    best = total[(0, 0)]
    denom = jnp.zeros((PEER_HEADS, tt), F32)
    counts = [jnp.zeros((PEER_HEADS, tt), F32) for _ in range(k)]
    for cell in cells:
        chosen = ahead[cell] < float(k)
        denom = denom + jnp.where(chosen, jnp.exp(total[cell] - best), 0.0)
        counts[cell[0]] = counts[cell[0]] + jnp.where(chosen, 1.0, 0.0)
    invz_scr[...] = 1.0 / denom
    for p in range(k):
        len_scr[p] = counts[p]

    key_iota = lax.broadcasted_iota(jnp.int32, (PEER_KEYS, tt), 0).astype(F32)
    for head in range(PEER_HEADS):
        hs = slice(head, head + 1)
        other = PEER_HEADS + head
        os_ = slice(other, other + 1)
        lrow = jnp.zeros((PEER_KEYS, tt), F32)
        rank2 = jnp.full((PEER_KEYS, tt), float(k), F32)
        for r in range(k):
            lrow = jnp.where(key_iota == idx_scr[r, hs, :], len_scr[r, hs, :], lrow)
            rank2 = jnp.where(key_iota == idx_scr[r, os_, :], float(r), rank2)
        lrow_ref[head] = lrow
        rank2_ref[head] = rank2.astype(rank2_ref.dtype)
        e1_ref[head] = jnp.exp(score_scr[head] - top_scr[0, hs, :]) * invz_scr[hs, :]
        e2_ref[head] = jnp.exp(score_scr[other] - top_scr[0, os_, :]).astype(e2_ref.dtype)


def _peer_route(xt_bf16, wq_t, keys):
    d_model, tokens = xt_bf16.shape
    tt = ROUTE_TOKENS
    qdim = wq_t.shape[0]
    slots = 2 * PEER_HEADS
    shape = (PEER_HEADS, PEER_KEYS, tokens)
    out_spec = pl.BlockSpec((PEER_HEADS, PEER_KEYS, tt), lambda t: (0, 0, t))
    return pl.pallas_call(
        _route_kernel,
        grid=(tokens // tt,),
        in_specs=[pl.BlockSpec((d_model, tt), lambda t: (0, t)),
                  pl.BlockSpec((qdim, d_model), lambda t: (0, 0)),
                  pl.BlockSpec((slots, PEER_KEYS, PEER_KEYS), lambda t: (0, 0, 0))],
        out_specs=[out_spec] * 4,
        out_shape=[jax.ShapeDtypeStruct(shape, BF16), jax.ShapeDtypeStruct(shape, BF16),
                   jax.ShapeDtypeStruct(shape, F32), jax.ShapeDtypeStruct(shape, F32)],
        scratch_shapes=[
            pltpu.VMEM((qdim, tt), BF16),
            pltpu.VMEM((slots, PEER_KEYS, tt), F32),
            pltpu.VMEM((slots * (tt // V7X_LANES), PEER_KEYS // V7X_SUBLANES, V7X_SUBLANES, V7X_LANES), F32),
            pltpu.VMEM((PEER_TOPK, slots, tt), F32),
            pltpu.VMEM((PEER_TOPK, slots, tt), F32),
            pltpu.VMEM((PEER_TOPK, PEER_HEADS, tt), F32),
            pltpu.VMEM((PEER_HEADS, tt), F32),
        ],
        compiler_params=pltpu.CompilerParams(
            dimension_semantics=("arbitrary",),
            vmem_limit_bytes=_vmem_limit(
                _nbytes((d_model, tt), BF16) + _nbytes((qdim, d_model), BF16)
                + _nbytes((slots, PEER_KEYS, PEER_KEYS), BF16) + 3 * _nbytes((PEER_HEADS, PEER_KEYS, tt), F32),
                _nbytes((qdim, tt), F32) + 4 * _nbytes((slots, PEER_KEYS, tt), F32)),
        ),
    )(xt_bf16, wq_t, keys)


def _expert_kernel(xtb_ref, xt_ref, u_ref, vt_ref, rank2_ref, e2_ref, lrow_ref, e1_ref, g_ref, b_ref,
                   o_ref, acc_ref, act_scr, w_scr, gate_scr, *, alpha):
    step = pl.program_id(1)
    eb, tt = w_scr.shape
    lane_tiles = tt // V7X_LANES
    packed = 2 * V7X_SUBLANES
    key_chunks = PEER_KEYS // packed

    @pl.when(step == 0)
    def _():
        acc_ref[...] = jnp.zeros_like(acc_ref)
        for rc in range(key_chunks):
            for lt in range(lane_tiles):
                for head in range(PEER_HEADS):
                    src = (head, slice(rc * packed, (rc + 1) * packed), slice(lt * V7X_LANES, (lt + 1) * V7X_LANES))
                    gate_scr[rc, lt, 2 * head] = rank2_ref[src]
                    gate_scr[rc, lt, 2 * head + 1] = e2_ref[src]

    sqrt_half = float(np.float32(np.sqrt(0.5)))
    zero = jnp.zeros((), BF16)

    for hb in range(eb // EXPERT_HIDDEN_ROWS):
        rows = slice(hb * EXPERT_HIDDEN_ROWS, (hb + 1) * EXPERT_HIDDEN_ROWS)
        hid = jnp.dot(u_ref[rows, :], xtb_ref[...], preferred_element_type=F32)
        act_scr[rows, :] = (0.5 * hid * (1.0 + lax.erf(hid * sqrt_half))).astype(BF16)
    for il in range(eb // PEER_KEYS):
        for lt in range(lane_tiles):
            ls = slice(lt * V7X_LANES, (lt + 1) * V7X_LANES)
            bcast = lambda ref, head: jnp.broadcast_to(ref[head, il:il + 1, ls], (packed, V7X_LANES)).astype(BF16)
            gate = [None] * key_chunks
            for head in range(PEER_HEADS):
                lrow = bcast(lrow_ref, head)
                e1 = bcast(e1_ref, head)
                for rc in range(key_chunks):
                    term = jnp.where(gate_scr[rc, lt, 2 * head] < lrow, gate_scr[rc, lt, 2 * head + 1], zero) * e1
                    gate[rc] = term if gate[rc] is None else gate[rc] + term
            for rc in range(key_chunks):
                ns = slice(il * PEER_KEYS + rc * packed, il * PEER_KEYS + (rc + 1) * packed)
                w_scr[ns, ls] = gate[rc] * act_scr[ns, ls]
    acc_ref[...] += jnp.dot(vt_ref[...], w_scr[...], preferred_element_type=F32)

    @pl.when(step == pl.num_programs(1) - 1)
    def _():
        y = alpha * xt_ref[...] + acc_ref[...]
        mu = jnp.mean(y, axis=0, keepdims=True)
        cen = y - mu
        var = jnp.mean(cen * cen, axis=0, keepdims=True)
        ln = cen * lax.rsqrt(var + LN_EPS) * g_ref[...] + b_ref[...]
        o_ref[...] = ln.T.astype(o_ref.dtype)


def _peer_experts(xt_bf16, xt_f32, u, v_t, rank2, e2, lrow, e1, ln_g, ln_b, alpha, out_dtype):
    d_model, tokens = xt_f32.shape
    experts = u.shape[0]
    tt = EXPERT_TOKENS
    eb = EXPERT_BLOCK
    keys_per_block = eb // PEER_KEYS
    full = pl.BlockSpec((PEER_HEADS, PEER_KEYS, tt), lambda t, e: (0, 0, t))
    rows = pl.BlockSpec((PEER_HEADS, keys_per_block, tt), lambda t, e: (0, e, t))
    col = pl.BlockSpec((d_model, 1), lambda t, e: (0, 0))
    return pl.pallas_call(
        functools.partial(_expert_kernel, alpha=alpha),
        grid=(tokens // tt, experts // eb),
        in_specs=[pl.BlockSpec((d_model, tt), lambda t, e: (0, t)),
                  pl.BlockSpec((d_model, tt), lambda t, e: (0, t)),
                  pl.BlockSpec((eb, d_model), lambda t, e: (e, 0)),
                  pl.BlockSpec((d_model, eb), lambda t, e: (0, e)),
                  full, full, rows, rows, col, col],
        out_specs=pl.BlockSpec((tt, d_model), lambda t, e: (t, 0)),
        out_shape=jax.ShapeDtypeStruct((tokens, d_model), out_dtype),
        scratch_shapes=[pltpu.VMEM((d_model, tt), F32),
                        pltpu.VMEM((eb, tt), BF16),
                        pltpu.VMEM((eb, tt), BF16),
                        pltpu.VMEM((PEER_KEYS // (2 * V7X_SUBLANES), tt // V7X_LANES, 2 * PEER_HEADS,
                                    2 * V7X_SUBLANES, V7X_LANES), BF16)],
        compiler_params=pltpu.CompilerParams(
            dimension_semantics=("arbitrary", "arbitrary"),
            vmem_limit_bytes=_vmem_limit(
                _nbytes((d_model, tt), BF16) + _nbytes((d_model, tt), F32) + 2 * _nbytes((eb, d_model), BF16)
                + 2 * _nbytes((PEER_HEADS, PEER_KEYS, tt), BF16) + 2 * _nbytes((PEER_HEADS, keys_per_block, tt), F32)
                + 2 * _nbytes((d_model, V7X_LANES), F32) + _nbytes((tt, d_model), F32),
                _nbytes((d_model, tt), F32) + 2 * _nbytes((eb, tt), BF16) + _nbytes((EXPERT_HIDDEN_ROWS, tt), F32)
                + 2 * _nbytes((PEER_HEADS, PEER_KEYS, tt), BF16) + 4 * _nbytes((d_model, tt), F32)),
        ),
    )(xt_bf16, xt_f32, u, v_t, rank2, e2, lrow, e1, ln_g, ln_b)


def _lambda_init(layer_idx):
    return 0.8 - 0.6 * math.exp(-0.3 * (layer_idx - 1))


def kernel(x, w_in, ret_gn_g, ret_gn_b, diff_lambda_q1, diff_lambda_k1, diff_lambda_q2, diff_lambda_k2,
           diff_subln_g, w_out, ln1_g, ln1_b, peer_w_query, peer_sub_keys, peer_u, peer_v, ln2_g, ln2_b):
    batch, seq, d_model = x.shape
    depth = w_in.shape[0]
    tokens = batch * seq
    alpha = (2.0 * depth) ** 0.25
    assert w_in.shape[2] == 7 * GROUP and d_model == 2 * GROUP
    assert seq % max(INPROJ_ROWS, RET_ROWS, DIFF_QBLOCK) == 0
    assert tokens % max(OUTPROJ_ROWS, ROUTE_TOKENS, EXPERT_TOKENS) == 0
    assert peer_u.shape[1] == PEER_KEYS * PEER_KEYS and peer_u.shape[1] % EXPERT_BLOCK == 0
    assert peer_sub_keys.shape[1:] == (PEER_HEADS, 2, PEER_KEYS, PEER_KEYS)

    cos_tab, sin_tab = _rope_tables(seq)
    x2d = x.reshape(tokens, d_model)
    for l in range(depth):
        lam_init = _lambda_init(l + 1)
        row = lambda p: p[l].reshape(1, -1)
        rq, rk, rv, rg, dq, dk, dv = _in_projection(x2d, w_in[l].astype(BF16), cos_tab, sin_tab, seq)
        r = _retention(rq, rk, rv, rg, row(ret_gn_g), row(ret_gn_b), batch, seq)
        d = _diff_attention(dq, dk, dv, row(diff_lambda_q1), row(diff_lambda_k1), row(diff_lambda_q2),
                            row(diff_lambda_k2), row(diff_subln_g), batch, seq, lam_init)
        xt_f32, xt_bf16 = _out_projection(r, d, x2d, w_out[l].astype(BF16), row(ln1_g), row(ln1_b), alpha)
        keys = peer_sub_keys[l].reshape(2 * PEER_HEADS, PEER_KEYS, PEER_KEYS).astype(BF16)
        rank2, e2, lrow, e1 = _peer_route(xt_bf16, peer_w_query[l].T.astype(BF16), keys)
        x2d = _peer_experts(xt_bf16, xt_f32, peer_u[l].astype(BF16), peer_v[l].T.astype(BF16),
                            rank2, e2, lrow, e1, ln2_g[l].reshape(-1, 1), ln2_b[l].reshape(-1, 1), alpha, x.dtype)
    return x2d.reshape(batch, seq, d_model)
```

```python
import functools
import math

import jax
import jax.numpy as jnp
import numpy as np
from jax import lax
from jax.experimental import pallas as pl
from jax.experimental.pallas import tpu as pltpu

F32 = jnp.float32
BF16 = jnp.bfloat16

V7X_LANES = 128
V7X_SUBLANES = 8
V7X_VMEM_BYTES = 64 * 1024 * 1024

RET_HEADS = 4
DIFF_HEADS = 4
HEAD_DIM = 128
RET_CHUNK = 128
ROPE_BASE = 10000.0
PEER_HEADS = 8
PEER_KEYS = 128
PEER_TOPK = 16
LN_EPS = 1e-5
GROUP = RET_HEADS * HEAD_DIM

INPROJ_ROWS = 512
RET_ROWS = 512
DIFF_QBLOCK = 512
OUTPROJ_ROWS = 512
ROUTE_TOKENS = 512
EXPERT_TOKENS = 512
EXPERT_BLOCK = 2048
EXPERT_HIDDEN_ROWS = 512


def _vmem_limit(pipelined_bytes, resident_bytes):
    need = 2 * pipelined_bytes + resident_bytes
    return int(min(max(need, 16 * 1024 * 1024), V7X_VMEM_BYTES - 8 * 1024 * 1024))


def _nbytes(shape, dtype):
    return int(np.prod(shape)) * jnp.dtype(dtype).itemsize


def _rope_kernel(cos_ref, sin_ref):
    shape = cos_ref.shape
    half = HEAD_DIM // 2
    pos = lax.broadcasted_iota(jnp.int32, shape, 0).astype(F32)
    lane = lax.broadcasted_iota(jnp.int32, shape, 1)
    freq_idx = jnp.where(lane >= half, lane - half, lane).astype(F32)
    inv = jnp.exp(freq_idx * (-math.log(ROPE_BASE) / half))
    ang = pos * inv
    cos_ref[...] = jnp.cos(ang)
    sin = jnp.sin(ang)
    sin_ref[...] = jnp.where(lane >= half, sin, -sin)


def _rope_tables(seq):
    return pl.pallas_call(
        _rope_kernel,
        out_shape=(jax.ShapeDtypeStruct((seq, HEAD_DIM), F32),) * 2,
    )()


def _inproj_kernel(x_ref, w_ref, cos_ref, sin_ref, rq_ref, rk_ref, rv_ref, rg_ref, dq_ref, dk_ref, dv_ref):
    xb = x_ref[...].astype(BF16)
    cos = cos_ref[...]
    sin = sin_ref[...]

    def proj(group):
        return jnp.dot(xb, w_ref[:, group * GROUP:(group + 1) * GROUP], preferred_element_type=F32)

    def store_rotary(out_ref, p, scale):
        for h in range(RET_HEADS):
            ph = p[:, h * HEAD_DIM:(h + 1) * HEAD_DIM]
            rot = ph * cos + pltpu.roll(ph, HEAD_DIM // 2, axis=1) * sin
            if scale != 1.0:
                rot = rot * scale
            out_ref[:, h * HEAD_DIM:(h + 1) * HEAD_DIM] = rot.astype(out_ref.dtype)

    store_rotary(rq_ref, proj(0), 1.0)
    store_rotary(rk_ref, proj(1), HEAD_DIM ** -0.5)
    rv_ref[...] = proj(2).astype(rv_ref.dtype)
    rg_ref[...] = proj(3).astype(rg_ref.dtype)
    dq_ref[...] = (proj(4) * ((HEAD_DIM // 2) ** -0.5)).astype(dq_ref.dtype)
    dk_ref[...] = proj(5).astype(dk_ref.dtype)
    dv_ref[...] = proj(6).astype(dv_ref.dtype)


def _in_projection(x2d, w_in, cos_tab, sin_tab, seq):
    tokens, d_model = x2d.shape
    cols = w_in.shape[1]
    tm = INPROJ_ROWS
    pos_blocks = seq // tm
    out = jax.ShapeDtypeStruct((tokens, GROUP), BF16)
    group_spec = pl.BlockSpec((tm, GROUP), lambda i: (i, 0))
    tab_spec = pl.BlockSpec((tm, HEAD_DIM), lambda i: (i % pos_blocks, 0))
    return pl.pallas_call(
        _inproj_kernel,
        grid=(tokens // tm,),
        in_specs=[
            pl.BlockSpec((tm, d_model), lambda i: (i, 0)),
            pl.BlockSpec((d_model, cols), lambda i: (0, 0)),
            tab_spec,
            tab_spec,
        ],
        out_specs=[group_spec] * 7,
        out_shape=[out] * 7,
        compiler_params=pltpu.CompilerParams(
            dimension_semantics=("arbitrary",),
            vmem_limit_bytes=_vmem_limit(
                _nbytes((tm, d_model), F32) + _nbytes((d_model, cols), BF16) + 7 * _nbytes((tm, GROUP), BF16)
                + 2 * _nbytes((tm, HEAD_DIM), F32),
                _nbytes((tm, d_model), BF16) + 3 * _nbytes((tm, GROUP), F32)),
        ),
    )(x2d, w_in, cos_tab, sin_tab)


def _retention_kernel(q_ref, k_ref, v_ref, g_ref, gn_g_ref, gn_b_ref, o_ref, state_ref):
    @pl.when(pl.program_id(1) == 0)
    def _():
        state_ref[...] = jnp.zeros_like(state_ref)

    c = RET_CHUNK
    row = lax.broadcasted_iota(jnp.int32, (c, c), 0).astype(F32)
    col = lax.broadcasted_iota(jnp.int32, (c, c), 1).astype(F32)
    rel = row - col
    idx = lax.broadcasted_iota(jnp.int32, (c, 1), 0).astype(F32)
    for h in range(RET_HEADS):
        log_g = math.log1p(-(2.0 ** (-5.0 - h)))
        decay_in = jnp.where(rel >= 0, jnp.exp(jnp.maximum(rel, 0.0) * log_g), 0.0)
        decay_q = jnp.exp((idx + 1.0) * log_g)
        decay_k = jnp.exp((c - 1.0 - idx) * log_g)
        decay_c = math.exp(c * log_g)
        hs = slice(h * HEAD_DIM, (h + 1) * HEAD_DIM)
        gn_g = gn_g_ref[:, hs]
        gn_b = gn_b_ref[:, hs]
        state = state_ref[h]
        for ci in range(q_ref.shape[0] // c):
            rs = slice(ci * c, (ci + 1) * c)
            q = q_ref[rs, hs]
            k = k_ref[rs, hs]
            v = v_ref[rs, hs]
            s = lax.dot_general(q, k, (((1,), (1,)), ((), ())), preferred_element_type=F32) * decay_in
            inner = jnp.dot(s.astype(BF16), v, preferred_element_type=F32)
            cross = jnp.dot(q, state.astype(BF16), preferred_element_type=F32) * decay_q
            kd_t = (k.astype(F32) * decay_k).T.astype(BF16)
            state = state * decay_c + jnp.dot(kd_t, v, preferred_element_type=F32)
            o = inner + cross
            mu = jnp.mean(o, axis=-1, keepdims=True)
            cen = o - mu
            var = jnp.mean(cen * cen, axis=-1, keepdims=True)
            normed = cen * lax.rsqrt(var + LN_EPS) * gn_g + gn_b
            gate = g_ref[rs, hs].astype(F32)
            o_ref[rs, hs] = (gate * jax.nn.sigmoid(gate) * normed).astype(o_ref.dtype)
        state_ref[h] = state


def _retention(rq, rk, rv, rg, gn_g, gn_b, batch, seq):
    rows = RET_ROWS
    blocks = seq // rows
    spec = pl.BlockSpec((rows, GROUP), lambda b, c: (b * blocks + c, 0))
    par = pl.BlockSpec((1, GROUP), lambda b, c: (0, 0))
    return pl.pallas_call(
        _retention_kernel,
        grid=(batch, blocks),
        in_specs=[spec, spec, spec, spec, par, par],
        out_specs=spec,
        out_shape=jax.ShapeDtypeStruct(rq.shape, BF16),
        scratch_shapes=[pltpu.VMEM((RET_HEADS, HEAD_DIM, HEAD_DIM), F32)],
        compiler_params=pltpu.CompilerParams(
            dimension_semantics=("arbitrary", "arbitrary"),
            vmem_limit_bytes=_vmem_limit(5 * _nbytes((rows, GROUP), BF16), 4 * 1024 * 1024),
        ),
    )(rq, rk, rv, rg, gn_g, gn_b)


def _diff_kernel(q_ref, k_ref, v_ref, lq1_ref, lk1_ref, lq2_ref, lk2_ref, g_ref, o_ref, *, lam_init):
    f = lambda r: r[...].astype(F32)
    lam = (jnp.exp(jnp.sum(f(lq1_ref) * f(lk1_ref), axis=-1, keepdims=True))
           - jnp.exp(jnp.sum(f(lq2_ref) * f(lk2_ref), axis=-1, keepdims=True)) + lam_init)
    seq = q_ref.shape[0]
    bq = DIFF_QBLOCK
    half = HEAD_DIM // 2
    lane = lax.broadcasted_iota(jnp.int32, (bq, HEAD_DIM), 1)
    tri = (lax.broadcasted_iota(jnp.int32, (bq, bq), 1) <= lax.broadcasted_iota(jnp.int32, (bq, bq), 0))
    nt = (((1,), (1,)), ((), ()))
    zero = jnp.zeros((), BF16)
    for qi in range(seq // bq):
        past = qi * bq
        q = q_ref[past:past + bq, :]
        q_comp = (jnp.where(lane < half, q, zero), jnp.where(lane >= half, q, zero))
        k_diag = k_ref[past:past + bq, :]
        attn_diag = None
        attn_past = None
        for comp in range(2):
            s_diag = lax.dot_general(q_comp[comp], k_diag, nt, preferred_element_type=F32)
            s_diag = jnp.where(tri, s_diag, -jnp.inf)
            m = jnp.max(s_diag, axis=-1, keepdims=True)
            if past:
                s_past = lax.dot_general(q_comp[comp], k_ref[0:past, :], nt, preferred_element_type=F32)
                m = jnp.maximum(m, jnp.max(s_past, axis=-1, keepdims=True))
                e_past = jnp.exp(s_past - m)
            e_diag = jnp.exp(s_diag - m)
            denom = jnp.sum(e_diag, axis=-1, keepdims=True)
            if past:
                denom = denom + jnp.sum(e_past, axis=-1, keepdims=True)
            weight = (1.0 / denom) if comp == 0 else (-lam / denom)
            attn_diag = e_diag * weight if comp == 0 else attn_diag + e_diag * weight
            if past:
                attn_past = e_past * weight if comp == 0 else attn_past + e_past * weight
        o = jnp.dot(attn_diag.astype(BF16), v_ref[past:past + bq, :], preferred_element_type=F32)
        if past:
            o = o + jnp.dot(attn_past.astype(BF16), v_ref[0:past, :], preferred_element_type=F32)
        normed = o * lax.rsqrt(jnp.mean(o * o, axis=-1, keepdims=True) + LN_EPS)
        o_ref[past:past + bq, :] = (normed * g_ref[...] * (1.0 - lam_init)).astype(o_ref.dtype)


def _diff_attention(dq, dk, dv, lq1, lk1, lq2, lk2, subln_g, batch, seq, lam_init):
    spec = pl.BlockSpec((seq, HEAD_DIM), lambda b, h: (b, h))
    vec = pl.BlockSpec((1, HEAD_DIM // 2), lambda b, h: (0, 0))
    return pl.pallas_call(
        functools.partial(_diff_kernel, lam_init=lam_init),
        grid=(batch, DIFF_HEADS),
        in_specs=[spec, spec, spec, vec, vec, vec, vec, pl.BlockSpec((1, HEAD_DIM), lambda b, h: (0, h))],
        out_specs=spec,
        out_shape=jax.ShapeDtypeStruct(dq.shape, BF16),
        compiler_params=pltpu.CompilerParams(
            dimension_semantics=("arbitrary", "arbitrary"),
            vmem_limit_bytes=_vmem_limit(4 * _nbytes((seq, HEAD_DIM), BF16),
                                         8 * _nbytes((DIFF_QBLOCK, seq), F32)),
        ),
    )(dq, dk, dv, lq1, lk1, lq2, lk2, subln_g)


def _outproj_kernel(r_ref, d_ref, x_ref, w_ref, g_ref, b_ref, xt_ref, xtb_ref, *, alpha):
    mix = (jnp.dot(r_ref[...], w_ref[0:GROUP, :], preferred_element_type=F32)
           + jnp.dot(d_ref[...], w_ref[GROUP:2 * GROUP, :], preferred_element_type=F32))
    y = alpha * x_ref[...] + mix
    mu = jnp.mean(y, axis=-1, keepdims=True)
    cen = y - mu
    var = jnp.mean(cen * cen, axis=-1, keepdims=True)
    ln = cen * lax.rsqrt(var + LN_EPS) * g_ref[...] + b_ref[...]
    ln_t = ln.T
    xt_ref[...] = ln_t
    xtb_ref[...] = ln_t.astype(xtb_ref.dtype)


def _out_projection(r, d, x2d, w_out, ln_g, ln_b, alpha):
    tokens, d_model = x2d.shape
    tm = OUTPROJ_ROWS
    half = pl.BlockSpec((tm, GROUP), lambda i: (i, 0))
    par = pl.BlockSpec((1, d_model), lambda i: (0, 0))
    t_spec = pl.BlockSpec((d_model, tm), lambda i: (0, i))
    return pl.pallas_call(
        functools.partial(_outproj_kernel, alpha=alpha),
        grid=(tokens // tm,),
        in_specs=[half, half, pl.BlockSpec((tm, d_model), lambda i: (i, 0)),
                  pl.BlockSpec((2 * GROUP, d_model), lambda i: (0, 0)), par, par],
        out_specs=[t_spec, t_spec],
        out_shape=[jax.ShapeDtypeStruct((d_model, tokens), F32), jax.ShapeDtypeStruct((d_model, tokens), BF16)],
        compiler_params=pltpu.CompilerParams(
            dimension_semantics=("arbitrary",),
            vmem_limit_bytes=_vmem_limit(
                2 * _nbytes((tm, GROUP), BF16) + _nbytes((tm, d_model), F32) + _nbytes((2 * GROUP, d_model), BF16)
                + _nbytes((d_model, tm), F32) + _nbytes((d_model, tm), BF16),
                4 * _nbytes((tm, d_model), F32)),
        ),
    )(r, d, x2d, w_out, ln_g, ln_b)


def _candidate_cells():
    return [(p, q) for p in range(PEER_TOPK) for q in range(PEER_TOPK) if (p + 1) * (q + 1) <= PEER_TOPK]


def _route_kernel(xt_ref, wq_ref, keys_ref, rank2_ref, e2_ref, lrow_ref, e1_ref,
                  q_scr, score_scr, work_scr, top_scr, idx_scr, len_scr, invz_scr):
    tt = xt_ref.shape[1]
    k = PEER_TOPK
    sub = V7X_SUBLANES
    groups = PEER_KEYS // sub
    lane_tiles = tt // V7X_LANES
    q_scr[...] = jnp.dot(wq_ref[...], xt_ref[...], preferred_element_type=F32).astype(BF16)
    group_iota = [(lax.broadcasted_iota(jnp.int32, (sub, V7X_LANES), 0) + g * sub).astype(F32) for g in range(groups)]

    chains = [(slot, lt) for slot in range(2 * PEER_HEADS) for lt in range(lane_tiles)]
    for slot in range(2 * PEER_HEADS):
        src = (slot % PEER_HEADS) * 2 + slot // PEER_HEADS
        scores = jnp.dot(keys_ref[src], q_scr[src * PEER_KEYS:(src + 1) * PEER_KEYS, :],
                         preferred_element_type=F32)
        score_scr[slot] = scores
        for lt in range(lane_tiles):
            for g in range(groups):
                work_scr[chains.index((slot, lt)), g] = scores[g * sub:(g + 1) * sub,
                                                               lt * V7X_LANES:(lt + 1) * V7X_LANES]

    def extract(r, carry):
        for c, (slot, lt) in enumerate(chains):
            ls = slice(lt * V7X_LANES, (lt + 1) * V7X_LANES)
            cur = [work_scr[c, g] for g in range(groups)]
            vals, idxs = list(cur), list(group_iota)
            while len(vals) > 1:
                left = [vals[a] >= vals[a + 1] for a in range(0, len(vals), 2)]
                idxs = [jnp.where(left[a // 2], idxs[a], idxs[a + 1]) for a in range(0, len(vals), 2)]
                vals = [jnp.maximum(vals[a], vals[a + 1]) for a in range(0, len(vals), 2)]
            v, i = vals[0], idxs[0]
            for shift in (4, 2, 1):
                v_other = pltpu.roll(v, shift, axis=0)
                i_other = pltpu.roll(i, shift, axis=0)
                keep = (v > v_other) | ((v == v_other) & (i < i_other))
                v = jnp.where(keep, v, v_other)
                i = jnp.where(keep, i, i_other)
            top_scr[r, slot:slot + 1, ls] = v[0:1, :]
            idx_scr[r, slot:slot + 1, ls] = i[0:1, :]
            for g in range(groups):
                work_scr[c, g] = jnp.where(group_iota[g] == i, -jnp.inf, cur[g])
        return carry

    lax.fori_loop(0, k, extract, 0)

    cells = _candidate_cells()
    a = [top_scr[p, 0:PEER_HEADS, :] for p in range(k)]
    b = [top_scr[q, PEER_HEADS:2 * PEER_HEADS, :] for q in range(k)]
    total = {cell: a[cell[0]] + b[cell[1]] for cell in cells}
    ahead = {cell: jnp.full((PEER_HEADS, tt), float((cell[0] + 1) * (cell[1] + 1) - 1), F32) for cell in cells}
    for xi, x in enumerate(cells):
        for y in cells[xi + 1:]:
            if (x[0] <= y[0] and x[1] <= y[1]) or (x[0] >= y[0] and x[1] >= y[1]):
                continue
            x_first = total[x] >= total[y]
            ahead[y] = ahead[y] + jnp.where(x_first, 1.0, 0.0)
            ahead[x] = ahead[x] + jnp.where(x_first, 0.0, 1.0)
    best = total[(0, 0)]
    denom = jnp.zeros((PEER_HEADS, tt), F32)
    counts = [jnp.zeros((PEER_HEADS, tt), F32) for _ in range(k)]
    for cell in cells:
        chosen = ahead[cell] < float(k)
        denom = denom + jnp.where(chosen, jnp.exp(total[cell] - best), 0.0)
        counts[cell[0]] = counts[cell[0]] + jnp.where(chosen, 1.0, 0.0)
    invz_scr[...] = 1.0 / denom
    for p in range(k):
        len_scr[p] = counts[p]

    key_iota = lax.broadcasted_iota(jnp.int32, (PEER_KEYS, tt), 0).astype(F32)
    for head in range(PEER_HEADS):
        hs = slice(head, head + 1)
        other = PEER_HEADS + head
        os_ = slice(other, other + 1)
        lrow = jnp.zeros((PEER_KEYS, tt), F32)
        rank2 = jnp.full((PEER_KEYS, tt), float(k), F32)
        for r in range(k):
            lrow = jnp.where(key_iota == idx_scr[r, hs, :], len_scr[r, hs, :], lrow)
            rank2 = jnp.where(key_iota == idx_scr[r, os_, :], float(r), rank2)
        lrow_ref[head] = lrow
        rank2_ref[head] = rank2.astype(rank2_ref.dtype)
        e1_ref[head] = jnp.exp(score_scr[head] - top_scr[0, hs, :]) * invz_scr[hs, :]
        e2_ref[head] = jnp.exp(score_scr[other] - top_scr[0, os_, :]).astype(e2_ref.dtype)


def _peer_route(xt_bf16, wq_t, keys):
    d_model, tokens = xt_bf16.shape
    tt = ROUTE_TOKENS
    qdim = wq_t.shape[0]
    slots = 2 * PEER_HEADS
    shape = (PEER_HEADS, PEER_KEYS, tokens)
    out_spec = pl.BlockSpec((PEER_HEADS, PEER_KEYS, tt), lambda t: (0, 0, t))
    return pl.pallas_call(
        _route_kernel,
        grid=(tokens // tt,),
        in_specs=[pl.BlockSpec((d_model, tt), lambda t: (0, t)),
                  pl.BlockSpec((qdim, d_model), lambda t: (0, 0)),
                  pl.BlockSpec((slots, PEER_KEYS, PEER_KEYS), lambda t: (0, 0, 0))],
        out_specs=[out_spec] * 4,
        out_shape=[jax.ShapeDtypeStruct(shape, BF16), jax.ShapeDtypeStruct(shape, BF16),
                   jax.ShapeDtypeStruct(shape, F32), jax.ShapeDtypeStruct(shape, F32)],
        scratch_shapes=[
            pltpu.VMEM((qdim, tt), BF16),
            pltpu.VMEM((slots, PEER_KEYS, tt), F32),
            pltpu.VMEM((slots * (tt // V7X_LANES), PEER_KEYS // V7X_SUBLANES, V7X_SUBLANES, V7X_LANES), F32),
            pltpu.VMEM((PEER_TOPK, slots, tt), F32),
            pltpu.VMEM((PEER_TOPK, slots, tt), F32),
            pltpu.VMEM((PEER_TOPK, PEER_HEADS, tt), F32),
            pltpu.VMEM((PEER_HEADS, tt), F32),
        ],
        compiler_params=pltpu.CompilerParams(
            dimension_semantics=("arbitrary",),
            vmem_limit_bytes=_vmem_limit(
                _nbytes((d_model, tt), BF16) + _nbytes((qdim, d_model), BF16)
                + _nbytes((slots, PEER_KEYS, PEER_KEYS), BF16) + 3 * _nbytes((PEER_HEADS, PEER_KEYS, tt), F32),
                _nbytes((qdim, tt), F32) + 4 * _nbytes((slots, PEER_KEYS, tt), F32)),
        ),
    )(xt_bf16, wq_t, keys)


def _expert_kernel(xtb_ref, xt_ref, u_ref, vt_ref, rank2_ref, e2_ref, lrow_ref, e1_ref, g_ref, b_ref,
                   o_ref, acc_ref, act_scr, w_scr, gate_scr, *, alpha):
    step = pl.program_id(1)
    eb, tt = w_scr.shape
    lane_tiles = tt // V7X_LANES
    packed = 2 * V7X_SUBLANES
    key_chunks = PEER_KEYS // packed

    @pl.when(step == 0)
    def _():
        acc_ref[...] = jnp.zeros_like(acc_ref)
        for rc in range(key_chunks):
            for lt in range(lane_tiles):
                for head in range(PEER_HEADS):
                    src = (head, slice(rc * packed, (rc + 1) * packed), slice(lt * V7X_LANES, (lt + 1) * V7X_LANES))
                    gate_scr[rc, lt, 2 * head] = rank2_ref[src]
                    gate_scr[rc, lt, 2 * head + 1] = e2_ref[src]

    sqrt_half = float(np.float32(np.sqrt(0.5)))
    zero = jnp.zeros((), BF16)

    for hb in range(eb // EXPERT_HIDDEN_ROWS):
        rows = slice(hb * EXPERT_HIDDEN_ROWS, (hb + 1) * EXPERT_HIDDEN_ROWS)
        hid = jnp.dot(u_ref[rows, :], xtb_ref[...], preferred_element_type=F32)
        act_scr[rows, :] = (0.5 * hid * (1.0 + lax.erf(hid * sqrt_half))).astype(BF16)
    for il in range(eb // PEER_KEYS):
        for lt in range(lane_tiles):
            ls = slice(lt * V7X_LANES, (lt + 1) * V7X_LANES)
            bcast = lambda ref, head: jnp.broadcast_to(ref[head, il:il + 1, ls], (packed, V7X_LANES)).astype(BF16)
            gate = [None] * key_chunks
            for head in range(PEER_HEADS):
                lrow = bcast(lrow_ref, head)
                e1 = bcast(e1_ref, head)
                for rc in range(key_chunks):
                    term = jnp.where(gate_scr[rc, lt, 2 * head] < lrow, gate_scr[rc, lt, 2 * head + 1], zero) * e1
                    gate[rc] = term if gate[rc] is None else gate[rc] + term
            for rc in range(key_chunks):
                ns = slice(il * PEER_KEYS + rc * packed, il * PEER_KEYS + (rc + 1) * packed)
                w_scr[ns, ls] = gate[rc] * act_scr[ns, ls]
    acc_ref[...] += jnp.dot(vt_ref[...], w_scr[...], preferred_element_type=F32)

    @pl.when(step == pl.num_programs(1) - 1)
    def _():
        y = alpha * xt_ref[...] + acc_ref[...]
        mu = jnp.mean(y, axis=0, keepdims=True)
        cen = y - mu
        var = jnp.mean(cen * cen, axis=0, keepdims=True)
        ln = cen * lax.rsqrt(var + LN_EPS) * g_ref[...] + b_ref[...]
        o_ref[...] = ln.T.astype(o_ref.dtype)


def _peer_experts(xt_bf16, xt_f32, u, v_t, rank2, e2, lrow, e1, ln_g, ln_b, alpha, out_dtype):
    d_model, tokens = xt_f32.shape
    experts = u.shape[0]
    tt = EXPERT_TOKENS
    eb = EXPERT_BLOCK
    keys_per_block = eb // PEER_KEYS
    full = pl.BlockSpec((PEER_HEADS, PEER_KEYS, tt), lambda t, e: (0, 0, t))
    rows = pl.BlockSpec((PEER_HEADS, keys_per_block, tt), lambda t, e: (0, e, t))
    col = pl.BlockSpec((d_model, 1), lambda t, e: (0, 0))
    return pl.pallas_call(
        functools.partial(_expert_kernel, alpha=alpha),
        grid=(tokens // tt, experts // eb),
        in_specs=[pl.BlockSpec((d_model, tt), lambda t, e: (0, t)),
                  pl.BlockSpec((d_model, tt), lambda t, e: (0, t)),
                  pl.BlockSpec((eb, d_model), lambda t, e: (e, 0)),
                  pl.BlockSpec((d_model, eb), lambda t, e: (0, e)),
                  full, full, rows, rows, col, col],
        out_specs=pl.BlockSpec((tt, d_model), lambda t, e: (t, 0)),
        out_shape=jax.ShapeDtypeStruct((tokens, d_model), out_dtype),
        scratch_shapes=[pltpu.VMEM((d_model, tt), F32),
                        pltpu.VMEM((eb, tt), BF16),
                        pltpu.VMEM((eb, tt), BF16),
                        pltpu.VMEM((PEER_KEYS // (2 * V7X_SUBLANES), tt // V7X_LANES, 2 * PEER_HEADS,
                                    2 * V7X_SUBLANES, V7X_LANES), BF16)],
        compiler_params=pltpu.CompilerParams(
            dimension_semantics=("arbitrary", "arbitrary"),
            vmem_limit_bytes=_vmem_limit(
                _nbytes((d_model, tt), BF16) + _nbytes((d_model, tt), F32) + 2 * _nbytes((eb, d_model), BF16)
                + 2 * _nbytes((PEER_HEADS, PEER_KEYS, tt), BF16) + 2 * _nbytes((PEER_HEADS, keys_per_block, tt), F32)
                + 2 * _nbytes((d_model, V7X_LANES), F32) + _nbytes((tt, d_model), F32),
                _nbytes((d_model, tt), F32) + 2 * _nbytes((eb, tt), BF16) + _nbytes((EXPERT_HIDDEN_ROWS, tt), F32)
                + 2 * _nbytes((PEER_HEADS, PEER_KEYS, tt), BF16) + 4 * _nbytes((d_model, tt), F32)),
        ),
    )(xt_bf16, xt_f32, u, v_t, rank2, e2, lrow, e1, ln_g, ln_b)


def _lambda_init(layer_idx):
    return 0.8 - 0.6 * math.exp(-0.3 * (layer_idx - 1))


def kernel(x, w_in, ret_gn_g, ret_gn_b, diff_lambda_q1, diff_lambda_k1, diff_lambda_q2, diff_lambda_k2,
           diff_subln_g, w_out, ln1_g, ln1_b, peer_w_query, peer_sub_keys, peer_u, peer_v, ln2_g, ln2_b):
    batch, seq, d_model = x.shape
    depth = w_in.shape[0]
    tokens = batch * seq
    alpha = (2.0 * depth) ** 0.25
    assert w_in.shape[2] == 7 * GROUP and d_model == 2 * GROUP
    assert seq % max(INPROJ_ROWS, RET_ROWS, DIFF_QBLOCK) == 0
    assert tokens % max(OUTPROJ_ROWS, ROUTE_TOKENS, EXPERT_TOKENS) == 0
    assert peer_u.shape[1] == PEER_KEYS * PEER_KEYS and peer_u.shape[1] % EXPERT_BLOCK == 0
    assert peer_sub_keys.shape[1:] == (PEER_HEADS, 2, PEER_KEYS, PEER_KEYS)

    cos_tab, sin_tab = _rope_tables(seq)
    x2d = x.reshape(tokens, d_model)
    for l in range(depth):
        lam_init = _lambda_init(l + 1)
        row = lambda p: p[l].reshape(1, -1)
        rq, rk, rv, rg, dq, dk, dv = _in_projection(x2d, w_in[l].astype(BF16), cos_tab, sin_tab, seq)
        r = _retention(rq, rk, rv, rg, row(ret_gn_g), row(ret_gn_b), batch, seq)
        d = _diff_attention(dq, dk, dv, row(diff_lambda_q1), row(diff_lambda_k1), row(diff_lambda_q2),
                            row(diff_lambda_k2), row(diff_subln_g), batch, seq, lam_init)
        xt_f32, xt_bf16 = _out_projection(r, d, x2d, w_out[l].astype(BF16), row(ln1_g), row(ln1_b), alpha)
        keys = peer_sub_keys[l].reshape(2 * PEER_HEADS, PEER_KEYS, PEER_KEYS).astype(BF16)
        rank2, e2, lrow, e1 = _peer_route(xt_bf16, peer_w_query[l].T.astype(BF16), keys)
        x2d = _peer_experts(xt_bf16, xt_f32, peer_u[l].astype(BF16), peer_v[l].T.astype(BF16),
                            rank2, e2, lrow, e1, ln2_g[l].reshape(-1, 1), ln2_b[l].reshape(-1, 1), alpha, x.dtype)
    return x2d.reshape(batch, seq, d_model)
```

```python
import functools
import math

import jax
import jax.numpy as jnp
import numpy as np
from jax import lax
from jax.experimental import pallas as pl
from jax.experimental.pallas import tpu as pltpu

F32 = jnp.float32
BF16 = jnp.bfloat16

V7X_LANES = 128
V7X_SUBLANES = 8
V7X_VMEM_BYTES = 64 * 1024 * 1024

RET_HEADS = 4
DIFF_HEADS = 4
HEAD_DIM = 128
RET_CHUNK = 128
ROPE_BASE = 10000.0
PEER_HEADS = 8
PEER_KEYS = 128
PEER_TOPK = 16
LN_EPS = 1e-5
GROUP = RET_HEADS * HEAD_DIM

INPROJ_ROWS = 512
RET_ROWS = 512
DIFF_QBLOCK = 512
OUTPROJ_ROWS = 512
ROUTE_TOKENS = 512
EXPERT_TOKENS = 512
EXPERT_BLOCK = 2048
EXPERT_HIDDEN_ROWS = 512


def _vmem_limit(pipelined_bytes, resident_bytes):
    need = 2 * pipelined_bytes + resident_bytes
    return int(min(max(need, 16 * 1024 * 1024), V7X_VMEM_BYTES - 8 * 1024 * 1024))


def _nbytes(shape, dtype):
    return int(np.prod(shape)) * jnp.dtype(dtype).itemsize


def _rope_kernel(cos_ref, sin_ref):
    shape = cos_ref.shape
    half = HEAD_DIM // 2
    pos = lax.broadcasted_iota(jnp.int32, shape, 0).astype(F32)
    lane = lax.broadcasted_iota(jnp.int32, shape, 1)
    freq_idx = jnp.where(lane >= half, lane - half, lane).astype(F32)
    inv = jnp.exp(freq_idx * (-math.log(ROPE_BASE) / half))
    ang = pos * inv
    cos_ref[...] = jnp.cos(ang)
    sin = jnp.sin(ang)
    sin_ref[...] = jnp.where(lane >= half, sin, -sin)


def _rope_tables(seq):
    return pl.pallas_call(
        _rope_kernel,
        out_shape=(jax.ShapeDtypeStruct((seq, HEAD_DIM), F32),) * 2,
    )()


def _inproj_kernel(x_ref, w_ref, cos_ref, sin_ref, rq_ref, rk_ref, rv_ref, rg_ref, dq_ref, dk_ref, dv_ref):
    xb = x_ref[...].astype(BF16)
    cos = cos_ref[...]
    sin = sin_ref[...]

    def proj(group):
        return jnp.dot(xb, w_ref[:, group * GROUP:(group + 1) * GROUP], preferred_element_type=F32)

    def store_rotary(out_ref, p, scale):
        for h in range(RET_HEADS):
            ph = p[:, h * HEAD_DIM:(h + 1) * HEAD_DIM]
            rot = ph * cos + pltpu.roll(ph, HEAD_DIM // 2, axis=1) * sin
            if scale != 1.0:
                rot = rot * scale
            out_ref[:, h * HEAD_DIM:(h + 1) * HEAD_DIM] = rot.astype(out_ref.dtype)

    store_rotary(rq_ref, proj(0), 1.0)
    store_rotary(rk_ref, proj(1), HEAD_DIM ** -0.5)
    rv_ref[...] = proj(2).astype(rv_ref.dtype)
    rg_ref[...] = proj(3).astype(rg_ref.dtype)
    dq_ref[...] = (proj(4) * ((HEAD_DIM // 2) ** -0.5)).astype(dq_ref.dtype)
    dk_ref[...] = proj(5).astype(dk_ref.dtype)
    dv_ref[...] = proj(6).astype(dv_ref.dtype)


def _in_projection(x2d, w_in, cos_tab, sin_tab, seq):
    tokens, d_model = x2d.shape
    cols = w_in.shape[1]
    tm = INPROJ_ROWS
    pos_blocks = seq // tm
    out = jax.ShapeDtypeStruct((tokens, GROUP), BF16)
    group_spec = pl.BlockSpec((tm, GROUP), lambda i: (i, 0))
    tab_spec = pl.BlockSpec((tm, HEAD_DIM), lambda i: (i % pos_blocks, 0))
    return pl.pallas_call(
        _inproj_kernel,
        grid=(tokens // tm,),
        in_specs=[
            pl.BlockSpec((tm, d_model), lambda i: (i, 0)),
            pl.BlockSpec((d_model, cols), lambda i: (0, 0)),
            tab_spec,
            tab_spec,
        ],
        out_specs=[group_spec] * 7,
        out_shape=[out] * 7,
        compiler_params=pltpu.CompilerParams(
            dimension_semantics=("arbitrary",),
            vmem_limit_bytes=_vmem_limit(
                _nbytes((tm, d_model), F32) + _nbytes((d_model, cols), BF16) + 7 * _nbytes((tm, GROUP), BF16)
                + 2 * _nbytes((tm, HEAD_DIM), F32),
                _nbytes((tm, d_model), BF16) + 3 * _nbytes((tm, GROUP), F32)),
        ),
    )(x2d, w_in, cos_tab, sin_tab)


def _retention_kernel(q_ref, k_ref, v_ref, g_ref, gn_g_ref, gn_b_ref, o_ref, state_ref):
    @pl.when(pl.program_id(1) == 0)
    def _():
        state_ref[...] = jnp.zeros_like(state_ref)

    c = RET_CHUNK
    row = lax.broadcasted_iota(jnp.int32, (c, c), 0).astype(F32)
    col = lax.broadcasted_iota(jnp.int32, (c, c), 1).astype(F32)
    rel = row - col
    idx = lax.broadcasted_iota(jnp.int32, (c, 1), 0).astype(F32)
    for h in range(RET_HEADS):
        log_g = math.log1p(-(2.0 ** (-5.0 - h)))
        decay_in = jnp.where(rel >= 0, jnp.exp(jnp.maximum(rel, 0.0) * log_g), 0.0)
        decay_q = jnp.exp((idx + 1.0) * log_g)
        decay_k = jnp.exp((c - 1.0 - idx) * log_g)
        decay_c = math.exp(c * log_g)
        hs = slice(h * HEAD_DIM, (h + 1) * HEAD_DIM)
        gn_g = gn_g_ref[:, hs]
        gn_b = gn_b_ref[:, hs]
        state = state_ref[h]
        for ci in range(q_ref.shape[0] // c):
            rs = slice(ci * c, (ci + 1) * c)
            q = q_ref[rs, hs]
            k = k_ref[rs, hs]
            v = v_ref[rs, hs]
            s = lax.dot_general(q, k, (((1,), (1,)), ((), ())), preferred_element_type=F32) * decay_in
            inner = jnp.dot(s.astype(BF16), v, preferred_element_type=F32)
            cross = jnp.dot(q, state.astype(BF16), preferred_element_type=F32) * decay_q
            kd_t = (k.astype(F32) * decay_k).T.astype(BF16)
            state = state * decay_c + jnp.dot(kd_t, v, preferred_element_type=F32)
            o = inner + cross
            mu = jnp.mean(o, axis=-1, keepdims=True)
            cen = o - mu
            var = jnp.mean(cen * cen, axis=-1, keepdims=True)
            normed = cen * lax.rsqrt(var + LN_EPS) * gn_g + gn_b
            gate = g_ref[rs, hs].astype(F32)
            o_ref[rs, hs] = (gate * jax.nn.sigmoid(gate) * normed).astype(o_ref.dtype)
        state_ref[h] = state


def _retention(rq, rk, rv, rg, gn_g, gn_b, batch, seq):
    rows = RET_ROWS
    blocks = seq // rows
    spec = pl.BlockSpec((rows, GROUP), lambda b, c: (b * blocks + c, 0))
    par = pl.BlockSpec((1, GROUP), lambda b, c: (0, 0))
    return pl.pallas_call(
        _retention_kernel,
        grid=(batch, blocks),
        in_specs=[spec, spec, spec, spec, par, par],
        out_specs=spec,
        out_shape=jax.ShapeDtypeStruct(rq.shape, BF16),
        scratch_shapes=[pltpu.VMEM((RET_HEADS, HEAD_DIM, HEAD_DIM), F32)],
        compiler_params=pltpu.CompilerParams(
            dimension_semantics=("arbitrary", "arbitrary"),
            vmem_limit_bytes=_vmem_limit(5 * _nbytes((rows, GROUP), BF16), 4 * 1024 * 1024),
        ),
    )(rq, rk, rv, rg, gn_g, gn_b)


def _diff_kernel(q_ref, k_ref, v_ref, lq1_ref, lk1_ref, lq2_ref, lk2_ref, g_ref, o_ref, *, lam_init):
    f = lambda r: r[...].astype(F32)
    lam = (jnp.exp(jnp.sum(f(lq1_ref) * f(lk1_ref), axis=-1, keepdims=True))
           - jnp.exp(jnp.sum(f(lq2_ref) * f(lk2_ref), axis=-1, keepdims=True)) + lam_init)
    seq = q_ref.shape[0]
    bq = DIFF_QBLOCK
    half = HEAD_DIM // 2
    lane = lax.broadcasted_iota(jnp.int32, (bq, HEAD_DIM), 1)
    tri = (lax.broadcasted_iota(jnp.int32, (bq, bq), 1) <= lax.broadcasted_iota(jnp.int32, (bq, bq), 0))
    nt = (((1,), (1,)), ((), ()))
    zero = jnp.zeros((), BF16)
    for qi in range(seq // bq):
        past = qi * bq
        q = q_ref[past:past + bq, :]
        q_comp = (jnp.where(lane < half, q, zero), jnp.where(lane >= half, q, zero))
        k_diag = k_ref[past:past + bq, :]
        attn_diag = None
        attn_past = None
        for comp in range(2):
            s_diag = lax.dot_general(q_comp[comp], k_diag, nt, preferred_element_type=F32)
            s_diag = jnp.where(tri, s_diag, -jnp.inf)
            m = jnp.max(s_diag, axis=-1, keepdims=True)
            if past:
                s_past = lax.dot_general(q_comp[comp], k_ref[0:past, :], nt, preferred_element_type=F32)
                m = jnp.maximum(m, jnp.max(s_past, axis=-1, keepdims=True))
                e_past = jnp.exp(s_past - m)
            e_diag = jnp.exp(s_diag - m)
            denom = jnp.sum(e_diag, axis=-1, keepdims=True)
            if past:
                denom = denom + jnp.sum(e_past, axis=-1, keepdims=True)
            weight = (1.0 / denom) if comp == 0 else (-lam / denom)
            attn_diag = e_diag * weight if comp == 0 else attn_diag + e_diag * weight
            if past:
                attn_past = e_past * weight if comp == 0 else attn_past + e_past * weight
        o = jnp.dot(attn_diag.astype(BF16), v_ref[past:past + bq, :], preferred_element_type=F32)
        if past:
            o = o + jnp.dot(attn_past.astype(BF16), v_ref[0:past, :], preferred_element_type=F32)
        normed = o * lax.rsqrt(jnp.mean(o * o, axis=-1, keepdims=True) + LN_EPS)
        o_ref[past:past + bq, :] = (normed * g_ref[...] * (1.0 - lam_init)).astype(o_ref.dtype)


def _diff_attention(dq, dk, dv, lq1, lk1, lq2, lk2, subln_g, batch, seq, lam_init):
    spec = pl.BlockSpec((seq, HEAD_DIM), lambda b, h: (b, h))
    vec = pl.BlockSpec((1, HEAD_DIM // 2), lambda b, h: (0, 0))
    return pl.pallas_call(
        functools.partial(_diff_kernel, lam_init=lam_init),
        grid=(batch, DIFF_HEADS),
        in_specs=[spec, spec, spec, vec, vec, vec, vec, pl.BlockSpec((1, HEAD_DIM), lambda b, h: (0, h))],
        out_specs=spec,
        out_shape=jax.ShapeDtypeStruct(dq.shape, BF16),
        compiler_params=pltpu.CompilerParams(
            dimension_semantics=("arbitrary", "arbitrary"),
            vmem_limit_bytes=_vmem_limit(4 * _nbytes((seq, HEAD_DIM), BF16),
                                         8 * _nbytes((DIFF_QBLOCK, seq), F32)),
        ),
    )(dq, dk, dv, lq1, lk1, lq2, lk2, subln_g)


def _outproj_kernel(r_ref, d_ref, x_ref, w_ref, g_ref, b_ref, xt_ref, xtb_ref, *, alpha):
    mix = (jnp.dot(r_ref[...], w_ref[0:GROUP, :], preferred_element_type=F32)
           + jnp.dot(d_ref[...], w_ref[GROUP:2 * GROUP, :], preferred_element_type=F32))
    y = alpha * x_ref[...] + mix
    mu = jnp.mean(y, axis=-1, keepdims=True)
    cen = y - mu
    var = jnp.mean(cen * cen, axis=-1, keepdims=True)
    ln = cen * lax.rsqrt(var + LN_EPS) * g_ref[...] + b_ref[...]
    ln_t = ln.T
    xt_ref[...] = ln_t
    xtb_ref[...] = ln_t.astype(xtb_ref.dtype)


def _out_projection(r, d, x2d, w_out, ln_g, ln_b, alpha):
    tokens, d_model = x2d.shape
    tm = OUTPROJ_ROWS
    half = pl.BlockSpec((tm, GROUP), lambda i: (i, 0))
    par = pl.BlockSpec((1, d_model), lambda i: (0, 0))
    t_spec = pl.BlockSpec((d_model, tm), lambda i: (0, i))
    return pl.pallas_call(
        functools.partial(_outproj_kernel, alpha=alpha),
        grid=(tokens // tm,),
        in_specs=[half, half, pl.BlockSpec((tm, d_model), lambda i: (i, 0)),
                  pl.BlockSpec((2 * GROUP, d_model), lambda i: (0, 0)), par, par],
        out_specs=[t_spec, t_spec],
        out_shape=[jax.ShapeDtypeStruct((d_model, tokens), F32), jax.ShapeDtypeStruct((d_model, tokens), BF16)],
        compiler_params=pltpu.CompilerParams(
            dimension_semantics=("arbitrary",),
            vmem_limit_bytes=_vmem_limit(
                2 * _nbytes((tm, GROUP), BF16) + _nbytes((tm, d_model), F32) + _nbytes((2 * GROUP, d_model), BF16)
                + _nbytes((d_model, tm), F32) + _nbytes((d_model, tm), BF16),
                4 * _nbytes((tm, d_model), F32)),
        ),
    )(r, d, x2d, w_out, ln_g, ln_b)


def _candidate_cells():
    return [(p, q) for p in range(PEER_TOPK) for q in range(PEER_TOPK) if (p + 1) * (q + 1) <= PEER_TOPK]


def _route_kernel(xt_ref, wq_ref, keys_ref, rank2_ref, e2_ref, lrow_ref, e1_ref,
                  q_scr, score_scr, work_scr, top_scr, idx_scr, len_scr, invz_scr):
    tt = xt_ref.shape[1]
    k = PEER_TOPK
    sub = V7X_SUBLANES
    groups = PEER_KEYS // sub
    lane_tiles = tt // V7X_LANES
    q_scr[...] = jnp.dot(wq_ref[...], xt_ref[...], preferred_element_type=F32).astype(BF16)
    group_iota = [(lax.broadcasted_iota(jnp.int32, (sub, V7X_LANES), 0) + g * sub).astype(F32) for g in range(groups)]

    chains = [(slot, lt) for slot in range(2 * PEER_HEADS) for lt in range(lane_tiles)]
    for slot in range(2 * PEER_HEADS):
        src = (slot % PEER_HEADS) * 2 + slot // PEER_HEADS
        scores = jnp.dot(keys_ref[src], q_scr[src * PEER_KEYS:(src + 1) * PEER_KEYS, :],
                         preferred_element_type=F32)
        score_scr[slot] = scores
        for lt in range(lane_tiles):
            for g in range(groups):
                work_scr[chains.index((slot, lt)), g] = scores[g * sub:(g + 1) * sub,
                                                               lt * V7X_LANES:(lt + 1) * V7X_LANES]

    def extract(r, carry):
        for c, (slot, lt) in enumerate(chains):
            ls = slice(lt * V7X_LANES, (lt + 1) * V7X_LANES)
            cur = [work_scr[c, g] for g in range(groups)]
            vals, idxs = list(cur), list(group_iota)
            while len(vals) > 1:
                left = [vals[a] >= vals[a + 1] for a in range(0, len(vals), 2)]
                idxs = [jnp.where(left[a // 2], idxs[a], idxs[a + 1]) for a in range(0, len(vals), 2)]
                vals = [jnp.maximum(vals[a], vals[a + 1]) for a in range(0, len(vals), 2)]
            v, i = vals[0], idxs[0]
            for shift in (4, 2, 1):
                v_other = pltpu.roll(v, shift, axis=0)
                i_other = pltpu.roll(i, shift, axis=0)
                keep = (v > v_other) | ((v == v_other) & (i < i_other))
                v = jnp.where(keep, v, v_other)
                i = jnp.where(keep, i, i_other)
            top_scr[r, slot:slot + 1, ls] = v[0:1, :]
            idx_scr[r, slot:slot + 1, ls] = i[0:1, :]
            for g in range(groups):
                work_scr[c, g] = jnp.where(group_iota[g] == i, -jnp.inf, cur[g])
        return carry

    lax.fori_loop(0, k, extract, 0)

    cells = _candidate_cells()
    a = [top_scr[p, 0:PEER_HEADS, :] for p in range(k)]
    b = [top_scr[q, PEER_HEADS:2 * PEER_HEADS, :] for q in range(k)]
    total = {cell: a[cell[0]] + b[cell[1]] for cell in cells}
    ahead = {cell: jnp.full((PEER_HEADS, tt), float((cell[0] + 1) * (cell[1] + 1) - 1), F32) for cell in cells}
    for xi, x in enumerate(cells):
        for y in cells[xi + 1:]:
            if (x[0] <= y[0] and x[1] <= y[1]) or (x[0] >= y[0] and x[1] >= y[1]):
                continue
            x_first = total[x] >= total[y]
            ahead[y] = ahead[y] + jnp.where(x_first, 1.0, 0.0)
            ahead[x] = ahead[x] + jnp.where(x_first, 0.0, 1.0)
    best = total[(0, 0)]
    denom = jnp.zeros((PEER_HEADS, tt), F32)
    counts = [jnp.zeros((PEER_HEADS, tt), F32) for _ in range(k)]
    for cell in cells:
        chosen = ahead[cell] < float(k)
        denom = denom + jnp.where(chosen, jnp.exp(total[cell] - best), 0.0)
        counts[cell[0]] = counts[cell[0]] + jnp.where(chosen, 1.0, 0.0)
    invz_scr[...] = 1.0 / denom
    for p in range(k):
        len_scr[p] = counts[p]

    key_iota = lax.broadcasted_iota(jnp.int32, (PEER_KEYS, tt), 0).astype(F32)
    for head in range(PEER_HEADS):
        hs = slice(head, head + 1)
        other = PEER_HEADS + head
        os_ = slice(other, other + 1)
        lrow = jnp.zeros((PEER_KEYS, tt), F32)
        rank2 = jnp.full((PEER_KEYS, tt), float(k), F32)
        for r in range(k):
            lrow = jnp.where(key_iota == idx_scr[r, hs, :], len_scr[r, hs, :], lrow)
            rank2 = jnp.where(key_iota == idx_scr[r, os_, :], float(r), rank2)
        lrow_ref[head] = lrow
        rank2_ref[head] = rank2.astype(rank2_ref.dtype)
        e1_ref[head] = jnp.exp(score_scr[head] - top_scr[0, hs, :]) * invz_scr[hs, :]
        e2_ref[head] = jnp.exp(score_scr[other] - top_scr[0, os_, :]).astype(e2_ref.dtype)


def _peer_route(xt_bf16, wq_t, keys):
    d_model, tokens = xt_bf16.shape
    tt = ROUTE_TOKENS
    qdim = wq_t.shape[0]
    slots = 2 * PEER_HEADS
    shape = (PEER_HEADS, PEER_KEYS, tokens)
    out_spec = pl.BlockSpec((PEER_HEADS, PEER_KEYS, tt), lambda t: (0, 0, t))
    return pl.pallas_call(
        _route_kernel,
        grid=(tokens // tt,),
        in_specs=[pl.BlockSpec((d_model, tt), lambda t: (0, t)),
                  pl.BlockSpec((qdim, d_model), lambda t: (0, 0)),
                  pl.BlockSpec((slots, PEER_KEYS, PEER_KEYS), lambda t: (0, 0, 0))],
        out_specs=[out_spec] * 4,
        out_shape=[jax.ShapeDtypeStruct(shape, BF16), jax.ShapeDtypeStruct(shape, BF16),
                   jax.ShapeDtypeStruct(shape, F32), jax.ShapeDtypeStruct(shape, F32)],
        scratch_shapes=[
            pltpu.VMEM((qdim, tt), BF16),
            pltpu.VMEM((slots, PEER_KEYS, tt), F32),
            pltpu.VMEM((slots * (tt // V7X_LANES), PEER_KEYS // V7X_SUBLANES, V7X_SUBLANES, V7X_LANES), F32),
            pltpu.VMEM((PEER_TOPK, slots, tt), F32),
            pltpu.VMEM((PEER_TOPK, slots, tt), F32),
            pltpu.VMEM((PEER_TOPK, PEER_HEADS, tt), F32),
            pltpu.VMEM((PEER_HEADS, tt), F32),
        ],
        compiler_params=pltpu.CompilerParams(
            dimension_semantics=("arbitrary",),
            vmem_limit_bytes=_vmem_limit(
                _nbytes((d_model, tt), BF16) + _nbytes((qdim, d_model), BF16)
                + _nbytes((slots, PEER_KEYS, PEER_KEYS), BF16) + 3 * _nbytes((PEER_HEADS, PEER_KEYS, tt), F32),
                _nbytes((qdim, tt), F32) + 4 * _nbytes((slots, PEER_KEYS, tt), F32)),
        ),
    )(xt_bf16, wq_t, keys)


def _expert_kernel(xtb_ref, xt_ref, u_ref, vt_ref, rank2_ref, e2_ref, lrow_ref, e1_ref, g_ref, b_ref,
                   o_ref, acc_ref, act_scr, w_scr, gate_scr, *, alpha):
    step = pl.program_id(1)
    eb, tt = w_scr.shape
    lane_tiles = tt // V7X_LANES
    packed = 2 * V7X_SUBLANES
    key_chunks = PEER_KEYS // packed

    @pl.when(step == 0)
    def _():
        acc_ref[...] = jnp.zeros_like(acc_ref)
        for rc in range(key_chunks):
            for lt in range(lane_tiles):
                for head in range(PEER_HEADS):
                    src = (head, slice(rc * packed, (rc + 1) * packed), slice(lt * V7X_LANES, (lt + 1) * V7X_LANES))
                    gate_scr[rc, lt, 2 * head] = rank2_ref[src]
                    gate_scr[rc, lt, 2 * head + 1] = e2_ref[src]

    sqrt_half = float(np.float32(np.sqrt(0.5)))
    zero = jnp.zeros((), BF16)

    for hb in range(eb // EXPERT_HIDDEN_ROWS):
        rows = slice(hb * EXPERT_HIDDEN_ROWS, (hb + 1) * EXPERT_HIDDEN_ROWS)
        hid = jnp.dot(u_ref[rows, :], xtb_ref[...], preferred_element_type=F32)
        act_scr[rows, :] = (0.5 * hid * (1.0 + lax.erf(hid * sqrt_half))).astype(BF16)
    for il in range(eb // PEER_KEYS):
        for lt in range(lane_tiles):
            ls = slice(lt * V7X_LANES, (lt + 1) * V7X_LANES)
            bcast = lambda ref, head: jnp.broadcast_to(ref[head, il:il + 1, ls], (packed, V7X_LANES)).astype(BF16)
            gate = [None] * key_chunks
            for head in range(PEER_HEADS):
                lrow = bcast(lrow_ref, head)
                e1 = bcast(e1_ref, head)
                for rc in range(key_chunks):
                    term = jnp.where(gate_scr[rc, lt, 2 * head] < lrow, gate_scr[rc, lt, 2 * head + 1], zero) * e1
                    gate[rc] = term if gate[rc] is None else gate[rc] + term
            for rc in range(key_chunks):
                ns = slice(il * PEER_KEYS + rc * packed, il * PEER_KEYS + (rc + 1) * packed)
                w_scr[ns, ls] = gate[rc] * act_scr[ns, ls]
    acc_ref[...] += lax.dot_general(vt_ref[...], w_scr[...], (((0,), (0,)), ((), ())), preferred_element_type=F32)

    @pl.when(step == pl.num_programs(1) - 1)
    def _():
        y = alpha * xt_ref[...] + acc_ref[...]
        mu = jnp.mean(y, axis=0, keepdims=True)
        cen = y - mu
        var = jnp.mean(cen * cen, axis=0, keepdims=True)
        ln = cen * lax.rsqrt(var + LN_EPS) * g_ref[...] + b_ref[...]
        o_ref[...] = ln.T.astype(o_ref.dtype)


def _peer_experts(xt_bf16, xt_f32, u, v_t, rank2, e2, lrow, e1, ln_g, ln_b, alpha, out_dtype):
    d_model, tokens = xt_f32.shape
    experts = u.shape[0]
    tt = EXPERT_TOKENS
    eb = EXPERT_BLOCK
    keys_per_block = eb // PEER_KEYS
    full = pl.BlockSpec((PEER_HEADS, PEER_KEYS, tt), lambda t, e: (0, 0, t))
    rows = pl.BlockSpec((PEER_HEADS, keys_per_block, tt), lambda t, e: (0, e, t))
    col = pl.BlockSpec((d_model, 1), lambda t, e: (0, 0))
    return pl.pallas_call(
        functools.partial(_expert_kernel, alpha=alpha),
        grid=(tokens // tt, experts // eb),
        in_specs=[pl.BlockSpec((d_model, tt), lambda t, e: (0, t)),
                  pl.BlockSpec((d_model, tt), lambda t, e: (0, t)),
                  pl.BlockSpec((eb, d_model), lambda t, e: (e, 0)),
                  pl.BlockSpec((eb, d_model), lambda t, e: (e, 0)),
                  full, full, rows, rows, col, col],
        out_specs=pl.BlockSpec((tt, d_model), lambda t, e: (t, 0)),
        out_shape=jax.ShapeDtypeStruct((tokens, d_model), out_dtype),
        scratch_shapes=[pltpu.VMEM((d_model, tt), F32),
                        pltpu.VMEM((eb, tt), BF16),
                        pltpu.VMEM((eb, tt), BF16),
                        pltpu.VMEM((PEER_KEYS // (2 * V7X_SUBLANES), tt // V7X_LANES, 2 * PEER_HEADS,
                                    2 * V7X_SUBLANES, V7X_LANES), BF16)],
        compiler_params=pltpu.CompilerParams(
            dimension_semantics=("arbitrary", "arbitrary"),
            vmem_limit_bytes=_vmem_limit(
                _nbytes((d_model, tt), BF16) + _nbytes((d_model, tt), F32) + 2 * _nbytes((eb, d_model), BF16)
                + 2 * _nbytes((PEER_HEADS, PEER_KEYS, tt), BF16) + 2 * _nbytes((PEER_HEADS, keys_per_block, tt), F32)
                + 2 * _nbytes((d_model, V7X_LANES), F32) + _nbytes((tt, d_model), F32),
                _nbytes((d_model, tt), F32) + 2 * _nbytes((eb, tt), BF16) + _nbytes((EXPERT_HIDDEN_ROWS, tt), F32)
                + 2 * _nbytes((PEER_HEADS, PEER_KEYS, tt), BF16) + 4 * _nbytes((d_model, tt), F32)),
        ),
    )(xt_bf16, xt_f32, u, v_t, rank2, e2, lrow, e1, ln_g, ln_b)


def _lambda_init(layer_idx):
    return 0.8 - 0.6 * math.exp(-0.3 * (layer_idx - 1))


def kernel(x, w_in, ret_gn_g, ret_gn_b, diff_lambda_q1, diff_lambda_k1, diff_lambda_q2, diff_lambda_k2,
           diff_subln_g, w_out, ln1_g, ln1_b, peer_w_query, peer_sub_keys, peer_u, peer_v, ln2_g, ln2_b):
    batch, seq, d_model = x.shape
    depth = w_in.shape[0]
    tokens = batch * seq
    alpha = (2.0 * depth) ** 0.25
    assert w_in.shape[2] == 7 * GROUP and d_model == 2 * GROUP
    assert seq % max(INPROJ_ROWS, RET_ROWS, DIFF_QBLOCK) == 0
    assert tokens % max(OUTPROJ_ROWS, ROUTE_TOKENS, EXPERT_TOKENS) == 0
    assert peer_u.shape[1] == PEER_KEYS * PEER_KEYS and peer_u.shape[1] % EXPERT_BLOCK == 0
    assert peer_sub_keys.shape[1:] == (PEER_HEADS, 2, PEER_KEYS, PEER_KEYS)

    cos_tab, sin_tab = _rope_tables(seq)
    x2d = x.reshape(tokens, d_model)
    for l in range(depth):
        lam_init = _lambda_init(l + 1)
        row = lambda p: p[l].reshape(1, -1)
        rq, rk, rv, rg, dq, dk, dv = _in_projection(x2d, w_in[l].astype(BF16), cos_tab, sin_tab, seq)
        r = _retention(rq, rk, rv, rg, row(ret_gn_g), row(ret_gn_b), batch, seq)
        d = _diff_attention(dq, dk, dv, row(diff_lambda_q1), row(diff_lambda_k1), row(diff_lambda_q2),
                            row(diff_lambda_k2), row(diff_subln_g), batch, seq, lam_init)
        xt_f32, xt_bf16 = _out_projection(r, d, x2d, w_out[l].astype(BF16), row(ln1_g), row(ln1_b), alpha)
        keys = peer_sub_keys[l].reshape(2 * PEER_HEADS, PEER_KEYS, PEER_KEYS).astype(BF16)
        rank2, e2, lrow, e1 = _peer_route(xt_bf16, peer_w_query[l].T.astype(BF16), keys)
        x2d = _peer_experts(xt_bf16, xt_f32, peer_u[l].astype(BF16), peer_v[l].astype(BF16),
                            rank2, e2, lrow, e1, ln2_g[l].reshape(-1, 1), ln2_b[l].reshape(-1, 1), alpha, x.dtype)
    return x2d.reshape(batch, seq, d_model)
```
